```python
import jax, jax.numpy as jnp
from jax import lax
import numpy as np

D_MODEL = 2048
BATCH = 1
SEQ = 8192
DEPTH = 2
DEC_BATCH = 128
DEC_SEQ = 4
PAST_LEN = 2048
PAGE_SIZE = 128

D_A = D_MODEL // 2
HD_A = 64
H_A = D_A // HD_A
LORA_W = 64
LORA_A = 64
LORA_G = 128
D_B = D_MODEL // 4
HD_B = 64
H_B = D_B // HD_B
D_C = D_MODEL - D_A - D_B
POOL_WINDOWS = (2, 4, 8, 16)
N_POOL_GROUPS = len(POOL_WINDOWS)
C_POOL = D_C // N_POOL_GROUPS
POOL_BUF = max(POOL_WINDOWS) - 1
RWKV_COLS = 3 * D_A + LORA_W + LORA_A + LORA_G
IN_COLS = RWKV_COLS + 3 * D_B + D_C
D_FF = -(-8 * D_MODEL // (3 * 256)) * 256
D_PLE = 256
SB_BLOCK = 128
EPS = 1e-6
GN_EPS = 64e-5

kernel_name = 'hybrid_rwkv7_stickbreak_pool_decode_step'


def rmsnorm(x, g):
    xf = x.astype(jnp.float32)
    y = xf * lax.rsqrt(jnp.mean(xf * xf, axis=-1, keepdims=True) + EPS)
    return (y * g.astype(jnp.float32)).astype(x.dtype)


def rwkv7_mixer(u, prev, wkv0, mu, w0, w_lora_up, a0, a_lora_up, g_lora_up,
                k_k, k_a, r_k, ln_w, ln_b):
    B, T, _ = u.shape
    f32 = jnp.float32
    uf = u.astype(f32)
    shifted = jnp.concatenate([prev.astype(f32)[:, None], uf[:, :-1]], axis=1)
    xs = uf + (shifted - uf) * mu
    i1, i2, i3 = D_A, 2 * D_A, 3 * D_A
    i4 = i3 + LORA_W
    i5 = i4 + LORA_A
    r, k, v, xw, xa, xg = jnp.split(xs, [i1, i2, i3, i4, i5], axis=-1)
    w = -jax.nn.softplus(-(w0 + jnp.tanh(xw) @ w_lora_up)) - 0.5
    decay = jnp.exp(-jnp.exp(w))
    a = jax.nn.sigmoid(a0 + xa @ a_lora_up)
    g = jax.nn.sigmoid(xg) @ g_lora_up
    heads = lambda z: z.reshape(B, T, H_A, HD_A)
    kk = heads(k * k_k)
    kk = kk / jnp.maximum(jnp.sqrt(jnp.sum(kk * kk, axis=-1, keepdims=True)), 1e-12)
    k = k * (1.0 + (a - 1.0) * k_a)
    r, k, v, decay, a = heads(r), heads(k), heads(v), heads(decay), heads(a)

    def step(S, inp):
        r_t, d_t, k_t, v_t, kk_t, a_t = inp
        sa = jnp.einsum('bhvk,bhk->bhv', S, kk_t)
        S = (S * d_t[:, :, None, :]
             - sa[..., None] * (kk_t * a_t)[:, :, None, :]
             + v_t[..., None] * k_t[:, :, None, :])
        return S, jnp.einsum('bhvk,bhk->bhv', S, r_t)

    seq = tuple(jnp.moveaxis(z, 1, 0) for z in (r, decay, k, v, kk, a))
    S_fin, ys = lax.scan(step, wkv0.astype(f32), seq)
    y = jnp.moveaxis(ys, 0, 1)
    mean = jnp.mean(y, axis=-1, keepdims=True)
    var = jnp.mean(jnp.square(y - mean), axis=-1, keepdims=True)
    yn = ((y - mean) * lax.rsqrt(var + GN_EPS)).reshape(B, T, D_A) * ln_w + ln_b
    bonus = jnp.sum(r * k * r_k, axis=-1, keepdims=True) * v
    out = (yn + bonus.reshape(B, T, D_A)) * g
    return out, S_fin


def stick_breaking(q, k, v, q_pos, k_pos, sb_bias):
    B, Tq, H, D = q.shape
    qb = SB_BLOCK if Tq % SB_BLOCK == 0 else Tq
    nb = Tq // qb
    qs = jnp.moveaxis(q.reshape(B, nb, qb, H, D), 1, 0)
    ps = q_pos.reshape(nb, qb)
    kf = k.astype(jnp.float32)
    vf = v.astype(jnp.float32)
    bias = sb_bias.astype(jnp.float32)[None, :, None, None]
    scale = D ** -0.5

    def block(args):
        qblk, pblk = args
        z = jnp.einsum('bqhd,bkhd->bhqk', qblk.astype(jnp.float32), kf) * scale + bias
        mask = k_pos[None, :] < pblk[:, None]
        log_keep = jnp.where(mask, jax.nn.log_sigmoid(-z), 0.0)
        later = lax.cumsum(log_keep, axis=3, reverse=True) - log_keep
        att = jnp.where(mask, jnp.exp(jax.nn.log_sigmoid(z) + later), 0.0)
        return jnp.einsum('bhqk,bkhd->bqhd', att, vf)

    out = lax.map(block, (qs, ps))
    return jnp.moveaxis(out, 0, 1).reshape(B, Tq, H * D)


def multiscale_pool(u, prefix, start_pos, w_pool, pool_scale):
    B, T, _ = u.shape
    f32 = jnp.float32
    uf = u.astype(f32)
    ext = jnp.concatenate([prefix.astype(f32), uf], axis=1)
    c = jnp.concatenate([jnp.zeros((B, 1, D_C), f32), jnp.cumsum(ext, axis=1)], axis=1)
    pos = start_pos + jnp.arange(T)
    end = c[:, POOL_BUF + 1:]
    pooled = []
    for gi, w in enumerate(POOL_WINDOWS):
        sl = slice(gi * C_POOL, (gi + 1) * C_POOL)
        start = c[:, POOL_BUF + 1 - w: POOL_BUF + 1 - w + T, sl]
        cnt = jnp.minimum(w, pos + 1).astype(f32)[None, :, None]
        pooled.append((end[..., sl] - start) / cnt - uf[..., sl])
    pooled = jnp.stack(pooled, axis=2)
    out = jnp.einsum('btgc,gcd->btgd', pooled, w_pool).reshape(B, T, D_C) * pool_scale
    return out, ext[:, -POOL_BUF:]


def decoder_layer(h, p_i, prev_shift, wkv0, pool_prefix, k_past, v_past, start_pos,
                  g_mix, w_in, mu_shift, w0, w_lora_up, a0, a_lora_up, g_lora_up,
                  k_k, k_a, r_k, ln_w, ln_b, sb_bias, w_pool, pool_scale, w_o,
                  g_ffn, w_gate, w_up, w_down, g_ple, w_pg, w_pp):
    B, T, _ = h.shape
    u = rmsnorm(h, g_mix) @ w_in
    u_a = u[..., :RWKV_COLS]
    u_b = u[..., RWKV_COLS:RWKV_COLS + 3 * D_B]
    u_c = u[..., RWKV_COLS + 3 * D_B:]
    y_a, wkv_new = rwkv7_mixer(u_a, prev_shift, wkv0, mu_shift, w0, w_lora_up, a0,
                               a_lora_up, g_lora_up, k_k, k_a, r_k, ln_w, ln_b)
    q, k, v = (z.reshape(B, T, H_B, HD_B) for z in jnp.split(u_b, 3, axis=-1))
    q_pos = start_pos + jnp.arange(T)
    if k_past is None:
        k_all, v_all, k_pos = k, v, q_pos
    else:
        k_all = jnp.concatenate([k_past.astype(k.dtype), k], axis=1)
        v_all = jnp.concatenate([v_past.astype(v.dtype), v], axis=1)
        k_pos = jnp.arange(k_past.shape[1] + T)
    y_b = stick_breaking(q, k_all, v_all, q_pos, k_pos, sb_bias)
    y_c, pool_new = multiscale_pool(u_c, pool_prefix, start_pos, w_pool, pool_scale)
    mix = jnp.concatenate([y_a, y_b, y_c], axis=-1).astype(h.dtype)
    h = h + mix @ w_o
    xn = rmsnorm(h, g_ffn)
    h = h + (jax.nn.silu(xn @ w_gate) * (xn @ w_up)) @ w_down
    xn = rmsnorm(h, g_ple)
    h = h + jax.nn.sigmoid(xn @ w_pg) * (p_i.astype(h.dtype) @ w_pp)
    return (h, k, v, wkv_new.astype(h.dtype), u_a[:, -1], pool_new.astype(h.dtype))


def setup_inputs(seed: int = 0) -> dict:
    key = jax.random.key(seed)
    ks = iter(jax.random.split(key, 48))
    f32 = jnp.float32
    nrm = lambda shape, s=1.0: jax.random.normal(next(ks), shape, f32) * s
    uni = lambda shape, lo, hi: jax.random.uniform(next(ks), shape, f32, lo, hi)
    n_pages = PAST_LEN // PAGE_SIZE
    n_pool = (DEC_BATCH * n_pages * 5 + 3) // 4
    page_table = jax.random.permutation(next(ks), n_pool)[:DEC_BATCH * n_pages]
    page_table = page_table.reshape(DEC_BATCH, n_pages).astype(jnp.int32)
    L = DEPTH
    return {
        'x_prompt': nrm((BATCH, SEQ, D_MODEL)),
        'x_sample': nrm((DEC_BATCH, DEC_SEQ, D_MODEL)),
        'p_prompt': nrm((DEPTH, BATCH, SEQ, D_PLE)),
        'p_sample': nrm((DEPTH, DEC_BATCH, DEC_SEQ, D_PLE)),
        'cache_k': nrm((DEPTH, n_pool, PAGE_SIZE, H_B, HD_B)),
        'cache_v': nrm((DEPTH, n_pool, PAGE_SIZE, H_B, HD_B)),
        'page_table': page_table,
        'state_wkv': nrm((DEPTH, DEC_BATCH, H_A, HD_A, HD_A), 0.3),
        'state_shift': nrm((DEPTH, DEC_BATCH, RWKV_COLS)),
        'state_pool': nrm((DEPTH, DEC_BATCH, POOL_BUF, D_C)),
        'g_mix': 1.0 + nrm((L, D_MODEL), 0.02),
        'w_in': nrm((L, D_MODEL, IN_COLS), D_MODEL ** -0.5),
        'mu_shift': uni((L, RWKV_COLS), 0.0, 1.0),
        'w0': uni((L, D_A), -5.0, 0.0),
        'w_lora_up': nrm((L, LORA_W, D_A), 0.5 * LORA_W ** -0.5),
        'a0': nrm((L, D_A), 0.5),
        'a_lora_up': nrm((L, LORA_A, D_A), 0.5 * LORA_A ** -0.5),
        'g_lora_up': nrm((L, LORA_G, D_A), LORA_G ** -0.5),
        'k_k': 1.0 + nrm((L, D_A), 0.1),
        'k_a': 1.0 + nrm((L, D_A), 0.1),
        'r_k': nrm((L, H_A, HD_A), 0.1),
        'ln_w': 1.0 + nrm((L, D_A), 0.02),
        'ln_b': nrm((L, D_A), 0.02),
        'sb_bias': uni((L, H_B), -8.5, -7.0),
        'w_pool': nrm((L, N_POOL_GROUPS, C_POOL, C_POOL), C_POOL ** -0.5),
        'pool_scale': 1.0 + nrm((L, D_C), 0.1),
        'w_o': nrm((L, D_MODEL, D_MODEL), D_MODEL ** -0.5),
        'g_ffn': 1.0 + nrm((L, D_MODEL), 0.02),
        'w_gate': nrm((L, D_MODEL, D_FF), D_MODEL ** -0.5),
        'w_up': nrm((L, D_MODEL, D_FF), D_MODEL ** -0.5),
        'w_down': nrm((L, D_FF, D_MODEL), D_FF ** -0.5),
        'g_ple': 1.0 + nrm((L, D_MODEL), 0.02),
        'w_pg': nrm((L, D_MODEL, D_MODEL), D_MODEL ** -0.5),
        'w_pp': nrm((L, D_PLE, D_MODEL), D_PLE ** -0.5),
        'g_final': 1.0 + nrm((D_MODEL,), 0.02),
    }


def reference(x_prompt, x_sample, p_prompt, p_sample, cache_k, cache_v, page_table,
              state_wkv, state_shift, state_pool, g_mix, w_in, mu_shift, w0, w_lora_up,
              a0, a_lora_up, g_lora_up, k_k, k_a, r_k, ln_w, ln_b, sb_bias, w_pool,
              pool_scale, w_o, g_ffn, w_gate, w_up, w_down, g_ple, w_pg, w_pp, g_final):
    n_pages = PAST_LEN // PAGE_SIZE
    bp = x_prompt.shape[0]
    bs = x_sample.shape[0]
    dt = x_prompt.dtype
    zero_shift = jnp.zeros((bp, RWKV_COLS), dt)
    zero_wkv = jnp.zeros((bp, H_A, HD_A, HD_A), dt)
    zero_pool = jnp.zeros((bp, POOL_BUF, D_C), dt)
    hp, hs = x_prompt, x_sample
    kp, vp, ksm, vsm, wp, ws, sp, ss, pp, ps = ([] for _ in range(10))
    for i in range(DEPTH):
        lw = (g_mix[i], w_in[i], mu_shift[i], w0[i], w_lora_up[i], a0[i], a_lora_up[i],
              g_lora_up[i], k_k[i], k_a[i], r_k[i], ln_w[i], ln_b[i], sb_bias[i], w_pool[i],
              pool_scale[i], w_o[i], g_ffn[i], w_gate[i], w_up[i], w_down[i], g_ple[i],
              w_pg[i], w_pp[i])
        hp, k_n, v_n, wkv_n, sh_n, pl_n = decoder_layer(
            hp, p_prompt[i], zero_shift, zero_wkv, zero_pool, None, None, 0, *lw)
        kp.append(k_n); vp.append(v_n); wp.append(wkv_n); sp.append(sh_n); pp.append(pl_n)
        k_past = cache_k[i][page_table].reshape(bs, n_pages * PAGE_SIZE, H_B, HD_B)
        v_past = cache_v[i][page_table].reshape(bs, n_pages * PAGE_SIZE, H_B, HD_B)
        hs, k_n, v_n, wkv_n, sh_n, pl_n = decoder_layer(
            hs, p_sample[i], state_shift[i], state_wkv[i], state_pool[i],
            k_past, v_past, PAST_LEN, *lw)
        ksm.append(k_n); vsm.append(v_n); ws.append(wkv_n); ss.append(sh_n); ps.append(pl_n)
    y_prompt = rmsnorm(hp, g_final)
    y_sample = rmsnorm(hs, g_final)
    return (y_prompt, y_sample,
            jnp.stack(kp), jnp.stack(vp), jnp.stack(ksm), jnp.stack(vsm),
            jnp.stack(wp), jnp.stack(ws), jnp.stack(sp), jnp.stack(ss),
            jnp.stack(pp), jnp.stack(ps))
```

```python
import functools

import jax
import jax.numpy as jnp
from jax import lax
from jax.experimental import pallas as pl
from jax.experimental.pallas import tpu as pltpu

F32 = jnp.float32
BF16 = jnp.bfloat16

D_MODEL = 2048
D_A = 1024
HD_A = 64
H_A = 16
LORA_W = 64
LORA_A = 64
LORA_G = 128
D_B = 512
HD_B = 64
H_B = 8
D_C = 512
POOL_WINDOWS = (2, 4, 8, 16)
C_POOL = 128
POOL_BUF = 15
RWKV_COLS = 3 * D_A + LORA_W + LORA_A + LORA_G
IN_COLS = RWKV_COLS + 3 * D_B + D_C
D_FF = 5632
D_PLE = 256
PAGE_SIZE = 128
EPS = 1e-6
GN_EPS = 64e-5

LANES = 128
MXU_DIM = 256
RWKV_CHUNK = 64
SB_BLOCK = 256
U_COL_BLOCK = 256
VMEM_LIMIT_MB = 56


def _params(sem, vmem_mb=VMEM_LIMIT_MB):
    return pltpu.CompilerParams(dimension_semantics=sem, vmem_limit_bytes=vmem_mb * 2**20)


def _dot(a, b):
    return jnp.dot(a, b, preferred_element_type=F32)


def _dot_nt(a, b):
    return lax.dot_general(a, b, (((1,), (1,)), ((), ())), preferred_element_type=F32)


def _split(x):
    hi = x.astype(BF16)
    lo = (x - hi.astype(F32)).astype(BF16)
    return hi, lo


def _dot_left_exact(m_bf16, x):
    hi, lo = _split(x)
    return _dot(m_bf16, hi) + _dot(m_bf16, lo)


def _dot_right_exact(x, m_bf16):
    hi, lo = _split(x)
    return _dot(hi, m_bf16) + _dot(lo, m_bf16)


def _softplus(z):
    return jnp.maximum(z, 0.0) + jnp.log1p(jnp.exp(-jnp.abs(z)))


def _sigmoid(x):
    return 1.0 / (1.0 + jnp.exp(-x))


def _rmsnorm_kernel(h_ref, g_ref, o_ref):
    x = h_ref[...]
    y = x * lax.rsqrt(jnp.mean(x * x, axis=-1, keepdims=True) + EPS)
    o_ref[...] = (y * g_ref[...]).astype(o_ref.dtype)


def _rmsnorm(h, g, out_dtype, tm):
    m, d = h.shape
    return pl.pallas_call(
        _rmsnorm_kernel,
        out_shape=jax.ShapeDtypeStruct((m, d), out_dtype),
        grid=(m // tm,),
        in_specs=[pl.BlockSpec((tm, d), lambda i: (i, 0)), pl.BlockSpec((1, d), lambda i: (0, 0))],
        out_specs=pl.BlockSpec((tm, d), lambda i: (i, 0)),
        compiler_params=_params(("parallel",)),
        name="rmsnorm",
    )(h, g.reshape(1, d))


def _dense_kernel(*refs, n_terms, acc_of, n_acc, has_res, epilogue):
    a_refs = refs[:n_terms]
    w_refs = refs[n_terms:2 * n_terms]
    pos = 2 * n_terms
    res_ref = refs[pos] if has_res else None
    pos += int(has_res)
    o_ref = refs[pos]
    wbf_refs = refs[pos + 1:]

    @pl.when(pl.program_id(1) == 0)
    def _():
        for t in range(n_terms):
            wbf_refs[t][...] = w_refs[t][...].astype(BF16)

    accs = [None] * n_acc
    for t in range(n_terms):
        d = _dot(a_refs[t][...], wbf_refs[t][...])
        accs[acc_of[t]] = d if accs[acc_of[t]] is None else accs[acc_of[t]] + d
    res = res_ref[...] if has_res else None
    o_ref[...] = epilogue(accs, res).astype(o_ref.dtype)


def _dense(terms, n_acc, epilogue, res, tm, tn, out_dtype, name):
    m = terms[0][0].shape[0]
    n = terms[0][1].shape[1]
    n_terms = len(terms)
    in_specs = [pl.BlockSpec((tm, t[2]), lambda j, i: (i, 0)) for t in terms]
    in_specs += [pl.BlockSpec((t[2], tn), functools.partial(lambda j, i, r: (r, j), r=t[3])) for t in terms]
    args = [t[0] for t in terms] + [t[1] for t in terms]
    if res is not None:
        in_specs.append(pl.BlockSpec((tm, tn), lambda j, i: (i, j)))
        args.append(res)
    kern = functools.partial(_dense_kernel, n_terms=n_terms, acc_of=[t[4] for t in terms], n_acc=n_acc,
                             has_res=res is not None, epilogue=epilogue)
    return pl.pallas_call(
        kern,
        out_shape=jax.ShapeDtypeStruct((m, n), out_dtype),
        grid=(n // tn, m // tm),
        in_specs=in_specs,
        out_specs=pl.BlockSpec((tm, tn), lambda j, i: (i, j)),
        scratch_shapes=[pltpu.VMEM((t[2], tn), BF16) for t in terms],
        compiler_params=_params(("arbitrary", "arbitrary")),
        name=name,
    )(*args)


def _epi_plain(accs, res):
    return accs[0]


def _epi_residual(accs, res):
    return res + accs[0]


def _epi_swiglu(accs, res):
    gate, up = accs
    return gate * _sigmoid(gate) * up


def _epi_gated_residual(accs, res):
    return res + _sigmoid(accs[0]) * accs[1]


def _seg_mats():
    row_head = lax.broadcasted_iota(jnp.int32, (D_A, LANES), 0) // HD_A
    col = lax.broadcasted_iota(jnp.int32, (D_A, LANES), 1)
    e1 = (row_head == col).astype(BF16)
    return e1, e1.T


def _segsum(x, e1, e2):
    return _dot_right_exact(_dot_right_exact(x, e1), e2)


def _rwkv_prep_kernel(u_ref, prev_ref, mu_ref, w0_ref, a0_ref, kk_ref, ka_ref, rk_ref,
                      wl_ref, al_ref, gl_ref, e1_ref, e2_ref,
                      r_o, k_o, v_o, kkn_o, kka_o, ld_o, g_o, bonus_o, carry_ref, *, lag):
    rows = u_ref.shape[0]

    @pl.when(pl.program_id(0) == 0)
    def _():
        carry_ref[...] = prev_ref[...]

    u = u_ref[...]
    if lag == 1:
        rolled = pltpu.roll(u, 1, axis=0)
        row = lax.broadcasted_iota(jnp.int32, u.shape, 0)
        shifted = jnp.where(row == 0, carry_ref[...], rolled)
        carry_ref[...] = u[rows - 1:rows, :]
    else:
        shifted = jnp.concatenate([carry_ref[...], u[:rows - lag, :]], axis=0)
        carry_ref[...] = u[rows - lag:, :]
    xs = u + (shifted - u) * mu_ref[...]

    r = xs[:, 0:D_A]
    k = xs[:, D_A:2 * D_A]
    v = xs[:, 2 * D_A:3 * D_A]
    x_wa = xs[:, 3 * D_A:3 * D_A + LORA_W + LORA_A]
    xg = xs[:, 3 * D_A + LORA_W + LORA_A:]
    lane = lax.broadcasted_iota(jnp.int32, x_wa.shape, 1)
    tw = jnp.where(lane < LORA_W, jnp.tanh(x_wa), 0.0).astype(BF16)
    xa = jnp.where(lane >= LORA_W, x_wa, 0.0).astype(BF16)
    w_lin = w0_ref[...] + _dot(tw, wl_ref[...].astype(BF16))
    w = -_softplus(-w_lin) - 0.5
    ld_o[...] = -jnp.exp(w)
    a = _sigmoid(a0_ref[...] + _dot(xa, al_ref[...].astype(BF16)))
    g_o[...] = _dot(_sigmoid(xg).astype(BF16), gl_ref[...].astype(BF16))

    e1 = e1_ref[...]
    e2 = e2_ref[...]
    kk = k * kk_ref[...]
    norm = jnp.maximum(jnp.sqrt(_segsum(kk * kk, e1, e2)), 1e-12)
    kkn = kk / norm
    k2 = k * (1.0 + (a - 1.0) * ka_ref[...])
    r_o[...] = r
    k_o[...] = k2
    v_o[...] = v
    kkn_o[...] = kkn
    kka_o[...] = kkn * a
    bonus_o[...] = _segsum(r * k2 * rk_ref[...], e1, e2) * v


def _rwkv_prep(u, row_block0, rows, prev, lag, tr, lw):
    e1, e2 = _seg_mats()
    zeros_w = jnp.zeros((LORA_W, D_A), F32)
    wl = jnp.concatenate([lw["w_lora_up"], zeros_w], axis=0)
    al = jnp.concatenate([zeros_w, lw["a_lora_up"]], axis=0)
    vec = lambda x: x.reshape(1, -1)
    const = lambda shape: pl.BlockSpec(shape, lambda i: (0, 0))
    out = jax.ShapeDtypeStruct((rows, D_A), F32)
    ospec = pl.BlockSpec((tr, D_A), lambda i: (i, 0))
    return pl.pallas_call(
        functools.partial(_rwkv_prep_kernel, lag=lag),
        out_shape=[out] * 8,
        grid=(rows // tr,),
        in_specs=[pl.BlockSpec((tr, RWKV_COLS), lambda i: (i + row_block0, 0)),
                  const((lag, RWKV_COLS)), const((1, RWKV_COLS)),
                  const((1, D_A)), const((1, D_A)), const((1, D_A)), const((1, D_A)), const((1, D_A)),
                  const((LORA_W + LORA_A, D_A)), const((LORA_W + LORA_A, D_A)), const((LORA_G, D_A)),
                  const((D_A, LANES)), const((LANES, D_A))],
        out_specs=[ospec] * 8,
        scratch_shapes=[pltpu.VMEM((lag, RWKV_COLS), F32)],
        compiler_params=_params(("arbitrary",)),
        name="rwkv_prep",
    )(u, prev, vec(lw["mu_shift"]), vec(lw["w0"]), vec(lw["a0"]), vec(lw["k_k"]), vec(lw["k_a"]),
      vec(lw["r_k"]), wl, al, lw["g_lora_up"], e1, e2)


def _rwkv_post_kernel(y_ref, bonus_ref, g_ref, lnw_ref, lnb_ref, e1_ref, e2_ref, o_ref):
    e1 = e1_ref[...]
    e2 = e2_ref[...]
    y = y_ref[...]
    mean = _segsum(y, e1, e2) * (1.0 / HD_A)
    d = y - mean
    var = _segsum(d * d, e1, e2) * (1.0 / HD_A)
    yn = d * lax.rsqrt(var + GN_EPS) * lnw_ref[...] + lnb_ref[...]
    o_ref[...] = ((yn + bonus_ref[...]) * g_ref[...]).astype(o_ref.dtype)


def _rwkv_post(y, bonus, g, lw, tr):
    rows = y.shape[0]
    e1, e2 = _seg_mats()
    rspec = pl.BlockSpec((tr, D_A), lambda i: (i, 0))
    const = lambda shape: pl.BlockSpec(shape, lambda i: (0, 0))
    return pl.pallas_call(
        _rwkv_post_kernel,
        out_shape=jax.ShapeDtypeStruct((rows, D_A), BF16),
        grid=(rows // tr,),
        in_specs=[rspec, rspec, rspec, const((1, D_A)), const((1, D_A)), const((D_A, LANES)), const((LANES, D_A))],
        out_specs=rspec,
        compiler_params=_params(("parallel",)),
        name="rwkv_post",
    )(y, bonus, g, lw["ln_w"].reshape(1, D_A), lw["ln_b"].reshape(1, D_A), e1, e2)


def _unit_lower_inverse(l_strict, n):
    row = lax.broadcasted_iota(jnp.int32, (n, n), 0)
    col = lax.broadcasted_iota(jnp.int32, (n, n), 1)
    eye = (row == col).astype(F32)
    t = eye + l_strict
    p = l_strict
    steps = max(n.bit_length() - 1, 0) if isinstance(n, int) else 0
    for _ in range(steps - 1):
        pb = p.astype(BF16)
        p = _dot(pb, pb)
        t = t + _dot(t.astype(BF16), p.astype(BF16))
    return t


def _rwkv_scan_kernel(r_ref, k_ref, v_ref, kkn_ref, kka_ref, ld_ref, tri_ref, y_ref, s_ref):
    c = RWKV_CHUNK

    @pl.when(pl.program_id(0) == 0)
    def _():
        s_ref[...] = jnp.zeros_like(s_ref)

    ld = ld_ref[...]
    cum = _dot_left_exact(tri_ref[...], ld)
    last = cum[c - 1:c, :]
    p_inc = jnp.exp(cum)
    p_exc = jnp.exp(cum - ld)
    p_inv = jnp.exp(-cum)
    p_end = jnp.exp(last - cum)
    p_tot = jnp.exp(last)
    kkn = kkn_ref[...]
    kka = kka_ref[...]
    k2 = k_ref[...]
    a_t = -kkn * p_exc
    r_t = r_ref[...] * p_inc
    b_t = kka * p_inv
    k_t = k2 * p_inv
    b_e = kka * p_end
    k_e = k2 * p_end
    v = v_ref[...]

    row = lax.broadcasted_iota(jnp.int32, (2 * c, 2 * c), 0)
    col = lax.broadcasted_iota(jnp.int32, (2 * c, 2 * c), 1)
    rr = row % c
    cc = col % c
    keep = cc < rr + jnp.where(row < c, 0, 1)
    zeros_cv = jnp.zeros((c, HD_A), F32)

    ys = []
    for h in range(H_A):
        sl = slice(h * HD_A, (h + 1) * HD_A)
        s0 = s_ref[h]
        ar = jnp.concatenate([a_t[:, sl], r_t[:, sl]], axis=0).astype(BF16)
        bk = jnp.concatenate([b_t[:, sl], k_t[:, sl]], axis=0).astype(BF16)
        gm = jnp.where(keep, _dot_nt(ar, bk), 0.0)
        gmb = gm.astype(BF16)
        ars = _dot_nt(ar, s0.astype(BF16))
        vh = v[:, sl]
        zv = jnp.concatenate([zeros_cv, vh], axis=0).astype(BF16)
        x = ars[:c] + _dot(gmb[:c, :], zv)
        tinv = _unit_lower_inverse(gm[:c, :c], c)
        w = _dot(tinv.astype(BF16), x.astype(BF16))
        wv = jnp.concatenate([w, vh], axis=0)
        ys.append(ars[c:] + _dot(gmb[c:, :], wv.astype(BF16)))
        be = jnp.concatenate([b_e[:, sl], k_e[:, sl]], axis=0).astype(BF16)
        s_ref[h] = s0 * p_tot[:, sl] + _dot(wv.T.astype(BF16), be)
    y_ref[...] = jnp.concatenate(ys, axis=1)


def _rwkv_scan_prompt(r, k2, v, kkn, kka, ld):
    t = r.shape[0]
    c = RWKV_CHUNK
    tri = (lax.broadcasted_iota(jnp.int32, (c, c), 0) >= lax.broadcasted_iota(jnp.int32, (c, c), 1)).astype(BF16)
    rspec = pl.BlockSpec((c, D_A), lambda i: (i, 0))
    return pl.pallas_call(
        _rwkv_scan_kernel,
        out_shape=[jax.ShapeDtypeStruct((t, D_A), F32), jax.ShapeDtypeStruct((H_A, HD_A, HD_A), F32)],
        grid=(t // c,),
        in_specs=[rspec] * 6 + [pl.BlockSpec((c, c), lambda i: (0, 0))],
        out_specs=[rspec, pl.BlockSpec((H_A, HD_A, HD_A), lambda i: (0, 0, 0))],
        compiler_params=_params(("arbitrary",)),
        name="rwkv_scan_prompt",
    )(r, k2, v, kkn, kka, ld, tri)


V_CHUNK = 8


def _rwkv_scan_sample_kernel(r_ref, k_ref, v_ref, kkn_ref, kka_ref, ld_ref, s_in_ref, y_ref, s_out_ref,
                             st_ref, yt_ref, xt_ref, *, steps, batch):
    n_tiles = 2 * HD_A * HD_A // LANES
    for c in range(n_tiles):
        tile = s_in_ref[:, c * LANES:(c + 1) * LANES].T
        st_ref[c // (n_tiles // 2), (c % (n_tiles // 2)) * 2:(c % (n_tiles // 2)) * 2 + 2] = (
            tile.reshape(2, HD_A, batch))
    for q, ref in enumerate((r_ref, k_ref, v_ref, kkn_ref, kka_ref, ld_ref)):
        for t in range(steps):
            x = ref[t * batch:(t + 1) * batch, :]
            if q == 5:
                x = jnp.exp(x)
            xt_ref[q, t] = x.T

    def chunk(i, carry):
        hh = i // (HD_A // V_CHUNK)
        v0 = pl.multiple_of((i % (HD_A // V_CHUNK)) * V_CHUNK, V_CHUNK)
        base = pl.multiple_of(hh * HD_A, HD_A)
        s = st_ref[hh, pl.ds(v0, V_CHUNK)]
        for t in range(steps):
            rt = xt_ref[0, t, pl.ds(base, HD_A), :]
            kt = xt_ref[1, t, pl.ds(base, HD_A), :]
            vt = xt_ref[2, t, pl.ds(base + v0, V_CHUNK), :]
            kkt = xt_ref[3, t, pl.ds(base, HD_A), :]
            kat = xt_ref[4, t, pl.ds(base, HD_A), :]
            dt = xt_ref[5, t, pl.ds(base, HD_A), :]
            sa = jnp.sum(s * kkt[None], axis=1)
            s = s * dt[None] - sa[:, None, :] * kat[None] + vt[:, None, :] * kt[None]
            yt_ref[t, pl.ds(base + v0, V_CHUNK), :] = jnp.sum(s * rt[None], axis=1)
        st_ref[hh, pl.ds(v0, V_CHUNK)] = s
        return carry

    lax.fori_loop(0, 2 * HD_A // V_CHUNK, chunk, 0)

    for t in range(steps):
        y_ref[t * batch:(t + 1) * batch, :] = yt_ref[t].T
    for c in range(n_tiles):
        hh = c // (n_tiles // 2)
        v0 = (c % (n_tiles // 2)) * 2
        tile = st_ref[hh, v0:v0 + 2].reshape(LANES, batch)
        s_out_ref[:, c * LANES:(c + 1) * LANES] = tile.T


def _rwkv_scan_sample(r, k2, v, kkn, kka, ld, state_all, layer, steps, batch):
    rows = r.shape[0]
    pair = 2 * HD_A * HD_A
    xspec = pl.BlockSpec((rows, LANES), lambda p: (0, p))
    return pl.pallas_call(
        functools.partial(_rwkv_scan_sample_kernel, steps=steps, batch=batch),
        out_shape=[jax.ShapeDtypeStruct((rows, D_A), F32),
                   jax.ShapeDtypeStruct((batch, H_A * HD_A * HD_A), F32)],
        grid=(H_A // 2,),
        in_specs=[xspec] * 6 + [pl.BlockSpec((None, batch, pair), lambda p: (layer, 0, p))],
        out_specs=[xspec, pl.BlockSpec((batch, pair), lambda p: (0, p))],
        scratch_shapes=[pltpu.VMEM((2, HD_A, HD_A, batch), F32),
                        pltpu.VMEM((steps, LANES, batch), F32),
                        pltpu.VMEM((6, steps, LANES, batch), F32)],
        compiler_params=_params(("parallel",)),
        name="rwkv_scan_sample",
    )(r, k2, v, kkn, kka, ld, state_all)


def _sb_prompt_kernel(bias_ref, q0_ref, q1_ref, k0_ref, k1_ref, v0_ref, v1_ref, u_ref, o_ref,
                      kt_ref, vb_ref, qm_ref, acc_ref, carry_ref):
    i = pl.program_id(0)
    tq = SB_BLOCK
    scale = HD_B ** -0.5
    n_pairs = H_B // 2

    k_blk = jnp.concatenate([k0_ref[...], k1_ref[...]], axis=1)
    kt_ref[i] = k_blk.T.astype(BF16)
    v_blk = jnp.concatenate([v0_ref[...], v1_ref[...]], axis=1)
    q_blk = jnp.concatenate([q0_ref[...], q1_ref[...]], axis=1) * scale
    lane = lax.broadcasted_iota(jnp.int32, (tq, LANES), 1)
    for p in range(n_pairs):
        vb_ref[i, p] = v_blk[:, p * LANES:(p + 1) * LANES].astype(BF16)
        qp = q_blk[:, p * LANES:(p + 1) * LANES]
        qm_ref[2 * p] = jnp.where(lane < HD_B, qp, 0.0).astype(BF16)
        qm_ref[2 * p + 1] = jnp.where(lane >= HD_B, qp, 0.0).astype(BF16)
    acc_ref[...] = jnp.zeros_like(acc_ref)

    row = lax.broadcasted_iota(jnp.int32, (tq, tq), 0)
    col = lax.broadcasted_iota(jnp.int32, (tq, tq), 1)
    causal = col < row

    def block(h, j, diag):
        p = lax.shift_right_logical(h, 1)
        kt = kt_ref[j, pl.ds(pl.multiple_of(p * LANES, LANES), LANES), :]
        z = _dot(qm_ref[h], kt) + bias_ref[h]
        sp = _softplus(z)
        lk = -sp
        if diag:
            lk = jnp.where(causal, lk, 0.0)
            before = jnp.zeros((tq, 1), F32)
        else:
            before = carry_ref[h][:, 0:1]
        later = _dot(lk.astype(BF16), u_ref[...]) + before
        att = jnp.exp(z - sp + later)
        if diag:
            att = jnp.where(causal, att, 0.0)
        total = before + jnp.sum(lk, axis=-1, keepdims=True)
        carry_ref[h] = jnp.broadcast_to(total, (tq, LANES))
        acc_ref[h] += _dot(att.astype(BF16), vb_ref[j, p])

    def heads(j, diag):
        def body(h, c):
            block(h, j, diag)
            return c
        lax.fori_loop(0, H_B, body, 0)

    heads(i, True)

    def off_diag(t, c):
        heads(i - 1 - t, False)
        return c

    lax.fori_loop(0, i, off_diag, 0)

    for p in range(n_pairs):
        o_ref[:, p * LANES:(p + 1) * LANES] = jnp.where(lane < HD_B, acc_ref[2 * p], acc_ref[2 * p + 1]).astype(
            o_ref.dtype)


def _sb_prompt(u, t, sb_bias):
    tq = SB_BLOCK
    nq = t // tq
    cb = RWKV_COLS // U_COL_BLOCK
    upper = (lax.broadcasted_iota(jnp.int32, (tq, tq), 0) > lax.broadcasted_iota(jnp.int32, (tq, tq), 1)).astype(BF16)
    uspec = lambda c: pl.BlockSpec((tq, U_COL_BLOCK), functools.partial(lambda i, c: (i, c), c=c))
    return pl.pallas_call(
        _sb_prompt_kernel,
        out_shape=jax.ShapeDtypeStruct((t, D_B), BF16),
        grid=(nq,),
        in_specs=[pl.BlockSpec(memory_space=pltpu.SMEM)] + [uspec(cb + c) for c in range(6)]
                 + [pl.BlockSpec((tq, tq), lambda i: (0, 0))],
        out_specs=pl.BlockSpec((tq, D_B), lambda i: (i, 0)),
        scratch_shapes=[pltpu.VMEM((nq, D_B, tq), BF16), pltpu.VMEM((nq, H_B // 2, tq, LANES), BF16),
                        pltpu.VMEM((H_B, tq, LANES), BF16), pltpu.VMEM((H_B, tq, LANES), F32),
                        pltpu.VMEM((H_B, tq, LANES), F32)],
        compiler_params=_params(("arbitrary",)),
        name="sb_prompt",
    )(sb_bias, u, u, u, u, u, u, upper)


Q_PAD = 8


def _sb_sample_kernel(pt_ref, bias_ref, qkv_ref, u_ref, *refs, n_pages, steps):
    k_refs = refs[:n_pages]
    v_refs = refs[n_pages:2 * n_pages]
    o_ref = refs[2 * n_pages]
    scale = HD_B ** -0.5
    rows = H_B * Q_PAD
    x = qkv_ref[0]
    q8 = x[:, 0:D_B] * scale
    zpad = jnp.zeros((PAGE_SIZE - Q_PAD, D_B), F32)
    k_new = jnp.concatenate([x[:, D_B:2 * D_B], zpad], axis=0).astype(BF16)
    v_new = jnp.concatenate([x[:, 2 * D_B:3 * D_B], zpad], axis=0).astype(BF16)
    lane_head = lax.broadcasted_iota(jnp.int32, (Q_PAD, D_B), 1) // HD_B
    qbd = jnp.concatenate([jnp.where(lane_head == h, q8, 0.0) for h in range(H_B)], axis=0).astype(BF16)
    bias = jnp.concatenate([jnp.full((Q_PAD, 1), bias_ref[h], F32) for h in range(H_B)], axis=0)

    z = jnp.concatenate([_dot_nt(qbd, k_refs[p][...].astype(BF16)) for p in range(n_pages)], axis=1) + bias
    zn = _dot_nt(qbd, k_new) + bias
    t_q = lax.broadcasted_iota(jnp.int32, (rows, PAGE_SIZE), 0) % Q_PAD
    c_k = lax.broadcasted_iota(jnp.int32, (rows, PAGE_SIZE), 1)
    new_ok = (c_k < t_q) & (c_k < steps)

    sp = _softplus(z)
    lk = -sp
    spn = _softplus(zn)
    lkn = jnp.where(new_ok, -spn, 0.0)
    upper = u_ref[...]
    later_n = _dot(lkn.astype(BF16), upper[:PAGE_SIZE, :PAGE_SIZE])
    att_n = jnp.where(new_ok, jnp.exp(zn - spn + later_n), 0.0)
    carry = jnp.sum(lkn, axis=-1, keepdims=True)
    blk = upper.shape[0]
    n_blk = n_pages * PAGE_SIZE // blk
    laters = [None] * n_blk
    for b in reversed(range(n_blk)):
        lkb = lk[:, b * blk:(b + 1) * blk]
        laters[b] = _dot(lkb.astype(BF16), upper) + carry
        carry = carry + jnp.sum(lkb, axis=-1, keepdims=True)
    att = jnp.exp(z - sp + jnp.concatenate(laters, axis=1)).astype(BF16)

    out = _dot(att_n.astype(BF16), v_new)
    for p in range(n_pages):
        out = out + _dot(att[:, p * PAGE_SIZE:(p + 1) * PAGE_SIZE], v_refs[p][...].astype(BF16))
    y = jnp.zeros((Q_PAD, D_B), F32)
    for h in range(H_B):
        y = y + jnp.where(lane_head == h, out[h * Q_PAD:(h + 1) * Q_PAD], 0.0)
    o_ref[0] = y


def _sb_sample(qkv_pad, cache_k, cache_v, page_table, sb_bias, layer, steps):
    batch, n_pages = page_table.shape
    blk = MXU_DIM
    upper = (lax.broadcasted_iota(jnp.int32, (blk, blk), 0) > lax.broadcasted_iota(jnp.int32, (blk, blk), 1)).astype(BF16)
    page = lambda p: pl.BlockSpec((None, None, PAGE_SIZE, D_B),
                                  functools.partial(lambda b, pt, p: (layer, pt[b, p], 0, 0), p=p))
    grid_spec = pltpu.PrefetchScalarGridSpec(
        num_scalar_prefetch=1,
        grid=(batch,),
        in_specs=[pl.BlockSpec(memory_space=pltpu.SMEM),
                  pl.BlockSpec((1, Q_PAD, 3 * D_B), lambda b, pt: (b, 0, 0)),
                  pl.BlockSpec((blk, blk), lambda b, pt: (0, 0))]
                 + [page(p) for p in range(n_pages)] * 2,
        out_specs=pl.BlockSpec((1, Q_PAD, D_B), lambda b, pt: (b, 0, 0)),
    )
    return pl.pallas_call(
        functools.partial(_sb_sample_kernel, n_pages=n_pages, steps=steps),
        out_shape=jax.ShapeDtypeStruct((batch, Q_PAD, D_B), F32),
        grid_spec=grid_spec,
        compiler_params=_params(("parallel",)),
        name="sb_sample",
    )(page_table, sb_bias, qkv_pad, upper, *([cache_k] * n_pages), *([cache_v] * n_pages))


def _pool_mix(sums, x, cnts, wp_ref, scale_ref):
    outs = []
    for g in range(len(POOL_WINDOWS)):
        pooled = sums[g] / cnts[g] - x[:, g * C_POOL:(g + 1) * C_POOL]
        outs.append(_dot(pooled.astype(BF16), wp_ref[g].astype(BF16)))
    return jnp.concatenate(outs, axis=1) * scale_ref[...]


def _pool_prompt_kernel(u0_ref, u1_ref, wp_ref, scale_ref, o_ref, ext_ref):
    tp = u0_ref.shape[0]
    pad = 16
    i = pl.program_id(0)

    @pl.when(i == 0)
    def _():
        ext_ref[0:pad, :] = jnp.zeros((pad, D_C), F32)

    x = jnp.concatenate([u0_ref[...], u1_ref[...]], axis=1)
    ext_ref[pad:pad + tp, :] = x
    pos = i * tp + lax.broadcasted_iota(jnp.int32, (tp, 1), 0)
    sums, cnts = [], []
    for g, w in enumerate(POOL_WINDOWS):
        s = x[:, g * C_POOL:(g + 1) * C_POOL]
        for back in range(1, w):
            s = s + ext_ref[pad - back:pad - back + tp, g * C_POOL:(g + 1) * C_POOL]
        sums.append(s)
        cnts.append(jnp.minimum(w, pos + 1).astype(F32))
    o_ref[...] = _pool_mix(sums, x, cnts, wp_ref, scale_ref).astype(o_ref.dtype)
    ext_ref[0:pad, :] = x[tp - pad:, :]


def _pool_prompt(u, t, w_pool, pool_scale, tp):
    cb = (RWKV_COLS + 3 * D_B) // U_COL_BLOCK
    uspec = lambda c: pl.BlockSpec((tp, U_COL_BLOCK), functools.partial(lambda i, c: (i, c), c=c))
    return pl.pallas_call(
        _pool_prompt_kernel,
        out_shape=jax.ShapeDtypeStruct((t, D_C), BF16),
        grid=(t // tp,),
        in_specs=[uspec(cb), uspec(cb + 1),
                  pl.BlockSpec((len(POOL_WINDOWS), C_POOL, C_POOL), lambda i: (0, 0, 0)),
                  pl.BlockSpec((1, D_C), lambda i: (0, 0))],
        out_specs=pl.BlockSpec((tp, D_C), lambda i: (i, 0)),
        scratch_shapes=[pltpu.VMEM((16 + tp, D_C), F32)],
        compiler_params=_params(("arbitrary",)),
        name="pool_prompt",
    )(u, u, w_pool, pool_scale.reshape(1, D_C))


def _pool_sample_kernel(u0_ref, u1_ref, pre_ref, wp_ref, scale_ref, o_ref, *, steps, batch, start_pos):
    x_all = jnp.concatenate([u0_ref[...], u1_ref[...]], axis=1)
    ext = [pre_ref[r] for r in range(POOL_BUF)] + [x_all[t * batch:(t + 1) * batch] for t in range(steps)]
    for t in range(steps):
        sums, cnts = [], []
        for g, w in enumerate(POOL_WINDOWS):
            s = ext[POOL_BUF + t][:, g * C_POOL:(g + 1) * C_POOL]
            for back in range(1, w):
                s = s + ext[POOL_BUF + t - back][:, g * C_POOL:(g + 1) * C_POOL]
            sums.append(s)
            cnts.append(float(min(w, start_pos + t + 1)))
        o_ref[t * batch:(t + 1) * batch, :] = _pool_mix(sums, ext[POOL_BUF + t], cnts, wp_ref, scale_ref).astype(
            o_ref.dtype)


def _pool_sample(u, row_block0, prefix_tm, w_pool, pool_scale, steps, batch, start_pos):
    rows = steps * batch
    cb = (RWKV_COLS + 3 * D_B) // U_COL_BLOCK
    uspec = lambda c: pl.BlockSpec((rows, U_COL_BLOCK), functools.partial(lambda i, c: (row_block0, c), c=c))
    return pl.pallas_call(
        functools.partial(_pool_sample_kernel, steps=steps, batch=batch, start_pos=start_pos),
        out_shape=jax.ShapeDtypeStruct((rows, D_C), BF16),
        grid=(1,),
        in_specs=[uspec(cb), uspec(cb + 1),
                  pl.BlockSpec((POOL_BUF, batch, D_C), lambda i: (0, 0, 0)),
                  pl.BlockSpec((len(POOL_WINDOWS), C_POOL, C_POOL), lambda i: (0, 0, 0)),
                  pl.BlockSpec((1, D_C), lambda i: (0, 0))],
        out_specs=pl.BlockSpec((rows, D_C), lambda i: (0, 0)),
        compiler_params=_params(("arbitrary",)),
        name="pool_sample",
    )(u, u, prefix_tm, w_pool, pool_scale.reshape(1, D_C))


TM_DENSE = 1088
TM_DOWN = 544


def kernel(x_prompt, x_sample, p_prompt, p_sample, cache_k, cache_v, page_table, state_wkv, state_shift, state_pool, g_mix, w_in, mu_shift, w0, w_lora_up, a0, a_lora_up, g_lora_up, k_k, k_a, r_k, ln_w, ln_b, sb_bias, w_pool, pool_scale, w_o, g_ffn, w_gate, w_up, w_down, g_ple, w_pg, w_pp, g_final):
    depth = w_in.shape[0]
    tp = x_prompt.shape[1]
    bs, ts = x_sample.shape[0], x_sample.shape[1]
    rows_s = bs * ts
    m = tp + rows_s
    n_pages = page_table.shape[1]
    past_len = n_pages * PAGE_SIZE

    to_tm = lambda x: jnp.swapaxes(x, 0, 1).reshape(rows_s, x.shape[-1])
    from_tm = lambda x: jnp.swapaxes(x.reshape(ts, bs, x.shape[-1]), 0, 1)

    h = jnp.concatenate([x_prompt[0], to_tm(x_sample)], axis=0)
    cache_k4 = cache_k.reshape(depth, -1, PAGE_SIZE, D_B)
    cache_v4 = cache_v.reshape(depth, -1, PAGE_SIZE, D_B)
    state_flat = state_wkv.reshape(depth, bs, H_A * HD_A * HD_A)

    outs = {n: [] for n in ("kp", "vp", "ks", "vs", "wp", "ws", "sp", "ss", "pp", "ps")}
    for i in range(depth):
        lw = dict(mu_shift=mu_shift[i], w0=w0[i], w_lora_up=w_lora_up[i], a0=a0[i], a_lora_up=a_lora_up[i],
                  g_lora_up=g_lora_up[i], k_k=k_k[i], k_a=k_a[i], r_k=r_k[i].reshape(-1), ln_w=ln_w[i], ln_b=ln_b[i])
        xn = _rmsnorm(h, g_mix[i], BF16, TM_DENSE)
        u = _dense([(xn, w_in[i], D_MODEL, 0, 0)], 1, _epi_plain, None, TM_DENSE, 768, F32, "dense_in")

        prep_p = _rwkv_prep(u, 0, tp, jnp.zeros((1, RWKV_COLS), F32), 1, 256, lw)
        y_p, wkv_p = _rwkv_scan_prompt(*prep_p[:6])
        ya_p = _rwkv_post(y_p, prep_p[7], prep_p[6], lw, 512)
        prep_s = _rwkv_prep(u, tp // rows_s, rows_s, state_shift[i], bs, rows_s, lw)
        y_s, wkv_s = _rwkv_scan_sample(*prep_s[:6], state_flat, i, ts, bs)
        ya_s = _rwkv_post(y_s, prep_s[7], prep_s[6], lw, rows_s)

        yb_p = _sb_prompt(u, tp, sb_bias[i])
        qkv_s = from_tm(u[tp:, RWKV_COLS:RWKV_COLS + 3 * D_B])
        qkv_pad = jnp.pad(qkv_s, ((0, 0), (0, Q_PAD - ts), (0, 0)))
        yb_s = _sb_sample(qkv_pad, cache_k4, cache_v4, page_table, sb_bias[i], i, ts)
        yb_s = to_tm(yb_s[:, :ts]).astype(BF16)

        yc_p = _pool_prompt(u, tp, w_pool[i], pool_scale[i], 512)
        yc_s = _pool_sample(u, tp // rows_s, jnp.swapaxes(state_pool[i], 0, 1), w_pool[i], pool_scale[i],
                            ts, bs, past_len)

        ya = jnp.concatenate([ya_p, ya_s], axis=0)
        yb = jnp.concatenate([yb_p, yb_s], axis=0)
        yc = jnp.concatenate([yc_p, yc_s], axis=0)
        h = _dense([(ya, w_o[i], D_A, 0, 0), (yb, w_o[i], D_B, D_A // D_B, 0),
                    (yc, w_o[i], D_C, (D_A + D_B) // D_C, 0)], 1, _epi_residual, h, TM_DENSE, 512, F32, "dense_o")

        xn = _rmsnorm(h, g_ffn[i], BF16, TM_DENSE)
        act = _dense([(xn, w_gate[i], D_MODEL, 0, 0), (xn, w_up[i], D_MODEL, 0, 1)], 2, _epi_swiglu, None,
                     TM_DENSE, 512, BF16, "dense_ffn_in")
        h = _dense([(act, w_down[i], D_FF, 0, 0)], 1, _epi_residual, h, TM_DOWN, 512, F32, "dense_ffn_out")

        xn = _rmsnorm(h, g_ple[i], BF16, TM_DENSE)
        p_i = jnp.concatenate([p_prompt[i, 0], to_tm(p_sample[i])], axis=0).astype(BF16)
        h = _dense([(xn, w_pg[i], D_MODEL, 0, 0), (p_i, w_pp[i], D_PLE, 0, 1)], 2, _epi_gated_residual, h,
                   TM_DENSE, 512, F32, "dense_ple")

        kv_p = u[:tp, RWKV_COLS + D_B:RWKV_COLS + 3 * D_B]
        outs["kp"].append(kv_p[:, :D_B].reshape(1, tp, H_B, HD_B))
        outs["vp"].append(kv_p[:, D_B:].reshape(1, tp, H_B, HD_B))
        outs["ks"].append(qkv_s[..., D_B:2 * D_B].reshape(bs, ts, H_B, HD_B))
        outs["vs"].append(qkv_s[..., 2 * D_B:].reshape(bs, ts, H_B, HD_B))
        outs["wp"].append(wkv_p.reshape(1, H_A, HD_A, HD_A))
        outs["ws"].append(wkv_s.reshape(bs, H_A, HD_A, HD_A))
        outs["sp"].append(u[tp - 1:tp, :RWKV_COLS])
        outs["ss"].append(u[m - bs:, :RWKV_COLS])
        uc_s = from_tm(u[tp:, RWKV_COLS + 3 * D_B:])
        outs["pp"].append(u[tp - POOL_BUF:tp, RWKV_COLS + 3 * D_B:][None])
        outs["ps"].append(jnp.concatenate([state_pool[i][:, ts:], uc_s], axis=1))

    y = _rmsnorm(h, g_final, F32, TM_DENSE)
    y_prompt = y[:tp][None]
    y_sample = from_tm(y[tp:])
    stack = lambda n: jnp.stack(outs[n])
    return (y_prompt, y_sample, stack("kp"), stack("vp"), stack("ks"), stack("vs"), stack("wp"), stack("ws"),
            stack("sp"), stack("ss"), stack("pp"), stack("ps"))
```

```python
import functools

import jax
import jax.numpy as jnp
from jax import lax
from jax.experimental import pallas as pl
from jax.experimental.pallas import tpu as pltpu

F32 = jnp.float32
BF16 = jnp.bfloat16

D_MODEL = 2048
D_A = 1024
HD_A = 64
H_A = 16
LORA_W = 64
LORA_A = 64
LORA_G = 128
D_B = 512
HD_B = 64
H_B = 8
D_C = 512
POOL_WINDOWS = (2, 4, 8, 16)
C_POOL = 128
POOL_BUF = 15
RWKV_COLS = 3 * D_A + LORA_W + LORA_A + LORA_G
IN_COLS = RWKV_COLS + 3 * D_B + D_C
D_FF = 5632
D_PLE = 256
PAGE_SIZE = 128
EPS = 1e-6
GN_EPS = 64e-5

LANES = 128
MXU_DIM = 256
RWKV_CHUNK = 64
SB_BLOCK = 256
U_COL_BLOCK = 256
VMEM_LIMIT_MB = 56
NEG_LOG2E = -1.4426950408889634


def _params(sem, vmem_mb=VMEM_LIMIT_MB):
    return pltpu.CompilerParams(dimension_semantics=sem, vmem_limit_bytes=vmem_mb * 2**20)


def _dot(a, b):
    return jnp.dot(a, b, preferred_element_type=F32)


def _dot_nt(a, b):
    return lax.dot_general(a, b, (((1,), (1,)), ((), ())), preferred_element_type=F32)


def _split(x):
    hi = x.astype(BF16)
    lo = (x - hi.astype(F32)).astype(BF16)
    return hi, lo


def _dot_left_exact(m_bf16, x):
    hi, lo = _split(x)
    return _dot(m_bf16, hi) + _dot(m_bf16, lo)


def _dot_right_exact(x, m_bf16):
    hi, lo = _split(x)
    return _dot(hi, m_bf16) + _dot(lo, m_bf16)


def _neg_softplus(z):
    e = jnp.exp2(jnp.abs(z) * NEG_LOG2E)
    return jnp.minimum(-z, 0.0) - jnp.log(1.0 + e)


def _sigmoid(x):
    return 1.0 / (1.0 + jnp.exp(-x))


def _rmsnorm_kernel(h_ref, g_ref, o_ref):
    x = h_ref[...]
    y = x * lax.rsqrt(jnp.mean(x * x, axis=-1, keepdims=True) + EPS)
    o_ref[...] = (y * g_ref[...]).astype(o_ref.dtype)


def _rmsnorm(h, g, out_dtype, tm):
    m, d = h.shape
    return pl.pallas_call(
        _rmsnorm_kernel,
        out_shape=jax.ShapeDtypeStruct((m, d), out_dtype),
        grid=(m // tm,),
        in_specs=[pl.BlockSpec((tm, d), lambda i: (i, 0)), pl.BlockSpec((1, d), lambda i: (0, 0))],
        out_specs=pl.BlockSpec((tm, d), lambda i: (i, 0)),
        compiler_params=_params(("parallel",)),
        name="rmsnorm",
    )(h, g.reshape(1, d))


def _dense_kernel(*refs, n_terms, acc_of, n_acc, has_res, epilogue):
    a_refs = refs[:n_terms]
    w_refs = refs[n_terms:2 * n_terms]
    pos = 2 * n_terms
    res_ref = refs[pos] if has_res else None
    pos += int(has_res)
    o_ref = refs[pos]
    wbf_refs = refs[pos + 1:]

    @pl.when(pl.program_id(1) == 0)
    def _():
        for t in range(n_terms):
            wbf_refs[t][...] = w_refs[t][...].astype(BF16)

    accs = [None] * n_acc
    for t in range(n_terms):
        d = _dot(a_refs[t][...], wbf_refs[t][...])
        accs[acc_of[t]] = d if accs[acc_of[t]] is None else accs[acc_of[t]] + d
    res = res_ref[...] if has_res else None
    o_ref[...] = epilogue(accs, res).astype(o_ref.dtype)


def _dense(terms, n_acc, epilogue, res, tm, tn, out_dtype, name):
    m = terms[0][0].shape[0]
    n = terms[0][1].shape[1]
    n_terms = len(terms)
    in_specs = [pl.BlockSpec((tm, t[2]), lambda j, i: (i, 0)) for t in terms]
    in_specs += [pl.BlockSpec((t[2], tn), functools.partial(lambda j, i, r: (r, j), r=t[3])) for t in terms]
    args = [t[0] for t in terms] + [t[1] for t in terms]
    if res is not None:
        in_specs.append(pl.BlockSpec((tm, tn), lambda j, i: (i, j)))
        args.append(res)
    kern = functools.partial(_dense_kernel, n_terms=n_terms, acc_of=[t[4] for t in terms], n_acc=n_acc,
                             has_res=res is not None, epilogue=epilogue)
    return pl.pallas_call(
        kern,
        out_shape=jax.ShapeDtypeStruct((m, n), out_dtype),
        grid=(n // tn, m // tm),
        in_specs=in_specs,
        out_specs=pl.BlockSpec((tm, tn), lambda j, i: (i, j)),
        scratch_shapes=[pltpu.VMEM((t[2], tn), BF16) for t in terms],
        compiler_params=_params(("arbitrary", "arbitrary")),
        name=name,
    )(*args)


def _epi_plain(accs, res):
    return accs[0]


def _epi_residual(accs, res):
    return res + accs[0]


def _epi_swiglu(accs, res):
    gate, up = accs
    return gate * _sigmoid(gate) * up


def _epi_gated_residual(accs, res):
    return res + _sigmoid(accs[0]) * accs[1]


def _seg_mats():
    row_head = lax.broadcasted_iota(jnp.int32, (D_A, LANES), 0) // HD_A
    col = lax.broadcasted_iota(jnp.int32, (D_A, LANES), 1)
    e1 = (row_head == col).astype(BF16)
    return e1, e1.T


def _segsum(x, e1, e2):
    return _dot_right_exact(_dot_right_exact(x, e1), e2)


def _rwkv_prep_kernel(u_ref, prev_ref, mu_ref, w0_ref, a0_ref, kk_ref, ka_ref, rk_ref,
                      wl_ref, al_ref, gl_ref, e1_ref, e2_ref,
                      r_o, k_o, v_o, kkn_o, kka_o, ld_o, g_o, bonus_o, carry_ref, *, lag):
    rows = u_ref.shape[0]

    @pl.when(pl.program_id(0) == 0)
    def _():
        carry_ref[...] = prev_ref[...]

    u = u_ref[...]
    if lag == 1:
        rolled = pltpu.roll(u, 1, axis=0)
        row = lax.broadcasted_iota(jnp.int32, u.shape, 0)
        shifted = jnp.where(row == 0, carry_ref[...], rolled)
        carry_ref[...] = u[rows - 1:rows, :]
    else:
        shifted = jnp.concatenate([carry_ref[...], u[:rows - lag, :]], axis=0)
        carry_ref[...] = u[rows - lag:, :]
    xs = u + (shifted - u) * mu_ref[...]

    r = xs[:, 0:D_A]
    k = xs[:, D_A:2 * D_A]
    v = xs[:, 2 * D_A:3 * D_A]
    x_wa = xs[:, 3 * D_A:3 * D_A + LORA_W + LORA_A]
    xg = xs[:, 3 * D_A + LORA_W + LORA_A:]
    lane = lax.broadcasted_iota(jnp.int32, x_wa.shape, 1)
    tw = jnp.where(lane < LORA_W, jnp.tanh(x_wa), 0.0).astype(BF16)
    xa = jnp.where(lane >= LORA_W, x_wa, 0.0).astype(BF16)
    w_lin = w0_ref[...] + _dot(tw, wl_ref[...].astype(BF16))
    w = _neg_softplus(-w_lin) - 0.5
    ld_o[...] = -jnp.exp(w)
    a = _sigmoid(a0_ref[...] + _dot(xa, al_ref[...].astype(BF16)))
    g_o[...] = _dot(_sigmoid(xg).astype(BF16), gl_ref[...].astype(BF16))

    e1 = e1_ref[...]
    e2 = e2_ref[...]
    kk = k * kk_ref[...]
    norm = jnp.maximum(jnp.sqrt(_segsum(kk * kk, e1, e2)), 1e-12)
    kkn = kk / norm
    k2 = k * (1.0 + (a - 1.0) * ka_ref[...])
    r_o[...] = r
    k_o[...] = k2
    v_o[...] = v
    kkn_o[...] = kkn
    kka_o[...] = kkn * a
    bonus_o[...] = _segsum(r * k2 * rk_ref[...], e1, e2) * v


def _rwkv_prep(u, row_block0, rows, prev, lag, tr, lw):
    e1, e2 = _seg_mats()
    zeros_w = jnp.zeros((LORA_W, D_A), F32)
    wl = jnp.concatenate([lw["w_lora_up"], zeros_w], axis=0)
    al = jnp.concatenate([zeros_w, lw["a_lora_up"]], axis=0)
    vec = lambda x: x.reshape(1, -1)
    const = lambda shape: pl.BlockSpec(shape, lambda i: (0, 0))
    out = jax.ShapeDtypeStruct((rows, D_A), F32)
    ospec = pl.BlockSpec((tr, D_A), lambda i: (i, 0))
    return pl.pallas_call(
        functools.partial(_rwkv_prep_kernel, lag=lag),
        out_shape=[out] * 8,
        grid=(rows // tr,),
        in_specs=[pl.BlockSpec((tr, RWKV_COLS), lambda i: (i + row_block0, 0)),
                  const((lag, RWKV_COLS)), const((1, RWKV_COLS)),
                  const((1, D_A)), const((1, D_A)), const((1, D_A)), const((1, D_A)), const((1, D_A)),
                  const((LORA_W + LORA_A, D_A)), const((LORA_W + LORA_A, D_A)), const((LORA_G, D_A)),
                  const((D_A, LANES)), const((LANES, D_A))],
        out_specs=[ospec] * 8,
        scratch_shapes=[pltpu.VMEM((lag, RWKV_COLS), F32)],
        compiler_params=_params(("arbitrary",)),
        name="rwkv_prep",
    )(u, prev, vec(lw["mu_shift"]), vec(lw["w0"]), vec(lw["a0"]), vec(lw["k_k"]), vec(lw["k_a"]),
      vec(lw["r_k"]), wl, al, lw["g_lora_up"], e1, e2)


def _rwkv_post_kernel(y_ref, bonus_ref, g_ref, lnw_ref, lnb_ref, e1_ref, e2_ref, o_ref):
    e1 = e1_ref[...]
    e2 = e2_ref[...]
    y = y_ref[...]
    mean = _segsum(y, e1, e2) * (1.0 / HD_A)
    d = y - mean
    var = _segsum(d * d, e1, e2) * (1.0 / HD_A)
    yn = d * lax.rsqrt(var + GN_EPS) * lnw_ref[...] + lnb_ref[...]
    o_ref[...] = ((yn + bonus_ref[...]) * g_ref[...]).astype(o_ref.dtype)


def _rwkv_post(y, bonus, g, lw, tr):
    rows = y.shape[0]
    e1, e2 = _seg_mats()
    rspec = pl.BlockSpec((tr, D_A), lambda i: (i, 0))
    const = lambda shape: pl.BlockSpec(shape, lambda i: (0, 0))
    return pl.pallas_call(
        _rwkv_post_kernel,
        out_shape=jax.ShapeDtypeStruct((rows, D_A), BF16),
        grid=(rows // tr,),
        in_specs=[rspec, rspec, rspec, const((1, D_A)), const((1, D_A)), const((D_A, LANES)), const((LANES, D_A))],
        out_specs=rspec,
        compiler_params=_params(("parallel",)),
        name="rwkv_post",
    )(y, bonus, g, lw["ln_w"].reshape(1, D_A), lw["ln_b"].reshape(1, D_A), e1, e2)


def _unit_lower_inverses(ls, n):
    row = lax.broadcasted_iota(jnp.int32, (n, n), 0)
    col = lax.broadcasted_iota(jnp.int32, (n, n), 1)
    eye = (row == col).astype(F32)
    ts = [eye + l for l in ls]
    ps = [l.astype(BF16) for l in ls]
    n_factors = n.bit_length() - 1
    for stage in range(1, n_factors):
        ps = [_dot(p, p).astype(BF16) for p in ps]
        ts = [t + _dot(t.astype(BF16), p) for t, p in zip(ts, ps)]
    return ts


def _rwkv_scan_kernel(r_ref, k_ref, v_ref, kkn_ref, kka_ref, ld_ref, tri_ref, y_ref, s_ref):
    c = RWKV_CHUNK
    heads = range(H_A)

    @pl.when(pl.program_id(0) == 0)
    def _():
        s_ref[...] = jnp.zeros_like(s_ref)

    ld = ld_ref[...]
    cum = _dot_left_exact(tri_ref[...], ld)
    last = cum[c - 1:c, :]
    p_inc = jnp.exp(cum)
    p_exc = jnp.exp(cum - ld)
    p_inv = jnp.exp(-cum)
    p_end = jnp.exp(last - cum)
    p_tot = jnp.exp(last)
    kkn = kkn_ref[...]
    kka = kka_ref[...]
    k2 = k_ref[...]
    a_t = -kkn * p_exc
    r_t = r_ref[...] * p_inc
    b_t = kka * p_inv
    k_t = k2 * p_inv
    b_e = kka * p_end
    k_e = k2 * p_end
    v = v_ref[...]

    row = lax.broadcasted_iota(jnp.int32, (2 * c, 2 * c), 0)
    col = lax.broadcasted_iota(jnp.int32, (2 * c, 2 * c), 1)
    rr = row % c
    cc = col % c
    keep = cc < rr + jnp.where(row < c, 0, 1)
    zeros_cv = jnp.zeros((c, HD_A), F32)

    sl = [slice(h * HD_A, (h + 1) * HD_A) for h in heads]
    s0 = [s_ref[h] for h in heads]
    ar = [jnp.concatenate([a_t[:, sl[h]], r_t[:, sl[h]]], axis=0).astype(BF16) for h in heads]
    bk = [jnp.concatenate([b_t[:, sl[h]], k_t[:, sl[h]]], axis=0).astype(BF16) for h in heads]
    gm = [jnp.where(keep, _dot_nt(ar[h], bk[h]), 0.0) for h in heads]
    ars = [_dot_nt(ar[h], s0[h].astype(BF16)) for h in heads]
    gmb = [g.astype(BF16) for g in gm]
    vh = [v[:, sl[h]] for h in heads]
    x = [ars[h][:c] + _dot(gmb[h][:c, :], jnp.concatenate([zeros_cv, vh[h]], axis=0).astype(BF16)) for h in heads]
    tinv = _unit_lower_inverses([g[:c, :c] for g in gm], c)
    w = [_dot(tinv[h].astype(BF16), x[h].astype(BF16)) for h in heads]
    wv = [jnp.concatenate([w[h], vh[h]], axis=0) for h in heads]
    ys = [ars[h][c:] + _dot(gmb[h][c:, :], wv[h].astype(BF16)) for h in heads]
    be = [jnp.concatenate([b_e[:, sl[h]], k_e[:, sl[h]]], axis=0).astype(BF16) for h in heads]
    s_new = [s0[h] * p_tot[:, sl[h]] + _dot(wv[h].T.astype(BF16), be[h]) for h in heads]
    for h in heads:
        s_ref[h] = s_new[h]
    y_ref[...] = jnp.concatenate(ys, axis=1)


def _rwkv_scan_prompt(r, k2, v, kkn, kka, ld):
    t = r.shape[0]
    c = RWKV_CHUNK
    tri = (lax.broadcasted_iota(jnp.int32, (c, c), 0) >= lax.broadcasted_iota(jnp.int32, (c, c), 1)).astype(BF16)
    rspec = pl.BlockSpec((c, D_A), lambda i: (i, 0))
    return pl.pallas_call(
        _rwkv_scan_kernel,
        out_shape=[jax.ShapeDtypeStruct((t, D_A), F32), jax.ShapeDtypeStruct((H_A, HD_A, HD_A), F32)],
        grid=(t // c,),
        in_specs=[rspec] * 6 + [pl.BlockSpec((c, c), lambda i: (0, 0))],
        out_specs=[rspec, pl.BlockSpec((H_A, HD_A, HD_A), lambda i: (0, 0, 0))],
        compiler_params=_params(("arbitrary",)),
        name="rwkv_scan_prompt",
    )(r, k2, v, kkn, kka, ld, tri)


V_CHUNK = 8


def _rwkv_scan_sample_kernel(r_ref, k_ref, v_ref, kkn_ref, kka_ref, ld_ref, s_in_ref, y_ref, s_out_ref,
                             st_ref, yt_ref, xt_ref, *, steps, batch):
    n_tiles = 2 * HD_A * HD_A // LANES
    for c in range(n_tiles):
        tile = s_in_ref[:, c * LANES:(c + 1) * LANES].T
        st_ref[c // (n_tiles // 2), (c % (n_tiles // 2)) * 2:(c % (n_tiles // 2)) * 2 + 2] = (
            tile.reshape(2, HD_A, batch))
    for q, ref in enumerate((r_ref, k_ref, v_ref, kkn_ref, kka_ref, ld_ref)):
        for t in range(steps):
            x = ref[t * batch:(t + 1) * batch, :]
            if q == 5:
                x = jnp.exp(x)
            xt_ref[q, t] = x.T

    def chunk(i, carry):
        hh = i // (HD_A // V_CHUNK)
        v0 = pl.multiple_of((i % (HD_A // V_CHUNK)) * V_CHUNK, V_CHUNK)
        base = pl.multiple_of(hh * HD_A, HD_A)
        s = st_ref[hh, pl.ds(v0, V_CHUNK)]
        for t in range(steps):
            rt = xt_ref[0, t, pl.ds(base, HD_A), :]
            kt = xt_ref[1, t, pl.ds(base, HD_A), :]
            vt = xt_ref[2, t, pl.ds(base + v0, V_CHUNK), :]
            kkt = xt_ref[3, t, pl.ds(base, HD_A), :]
            kat = xt_ref[4, t, pl.ds(base, HD_A), :]
            dt = xt_ref[5, t, pl.ds(base, HD_A), :]
            sa = jnp.sum(s * kkt[None], axis=1)
            s = s * dt[None] - sa[:, None, :] * kat[None] + vt[:, None, :] * kt[None]
            yt_ref[t, pl.ds(base + v0, V_CHUNK), :] = jnp.sum(s * rt[None], axis=1)
        st_ref[hh, pl.ds(v0, V_CHUNK)] = s
        return carry

    lax.fori_loop(0, 2 * HD_A // V_CHUNK, chunk, 0)

    for t in range(steps):
        y_ref[t * batch:(t + 1) * batch, :] = yt_ref[t].T
    for c in range(n_tiles):
        hh = c // (n_tiles // 2)
        v0 = (c % (n_tiles // 2)) * 2
        tile = st_ref[hh, v0:v0 + 2].reshape(LANES, batch)
        s_out_ref[:, c * LANES:(c + 1) * LANES] = tile.T


def _rwkv_scan_sample(r, k2, v, kkn, kka, ld, state_all, layer, steps, batch):
    rows = r.shape[0]
    pair = 2 * HD_A * HD_A
    xspec = pl.BlockSpec((rows, LANES), lambda p: (0, p))
    return pl.pallas_call(
        functools.partial(_rwkv_scan_sample_kernel, steps=steps, batch=batch),
        out_shape=[jax.ShapeDtypeStruct((rows, D_A), F32),
                   jax.ShapeDtypeStruct((batch, H_A * HD_A * HD_A), F32)],
        grid=(H_A // 2,),
        in_specs=[xspec] * 6 + [pl.BlockSpec((None, batch, pair), lambda p: (layer, 0, p))],
        out_specs=[xspec, pl.BlockSpec((batch, pair), lambda p: (0, p))],
        scratch_shapes=[pltpu.VMEM((2, HD_A, HD_A, batch), F32),
                        pltpu.VMEM((steps, LANES, batch), F32),
                        pltpu.VMEM((6, steps, LANES, batch), F32)],
        compiler_params=_params(("parallel",)),
        name="rwkv_scan_sample",
    )(r, k2, v, kkn, kka, ld, state_all)


def _sb_prompt_kernel(bias_ref, q0_ref, q1_ref, k0_ref, k1_ref, v0_ref, v1_ref, u_ref, o_ref,
                      kt_ref, vb_ref, qm_ref, acc_ref, carry_ref):
    i = pl.program_id(0)
    tq = SB_BLOCK
    scale = HD_B ** -0.5
    n_pairs = H_B // 2

    k_blk = jnp.concatenate([k0_ref[...], k1_ref[...]], axis=1)
    kt_ref[i] = k_blk.T.astype(BF16)
    v_blk = jnp.concatenate([v0_ref[...], v1_ref[...]], axis=1)
    q_blk = jnp.concatenate([q0_ref[...], q1_ref[...]], axis=1) * scale
    lane = lax.broadcasted_iota(jnp.int32, (tq, LANES), 1)
    for p in range(n_pairs):
        vb_ref[i, p] = v_blk[:, p * LANES:(p + 1) * LANES].astype(BF16)
        qp = q_blk[:, p * LANES:(p + 1) * LANES]
        qm_ref[p, 0:tq] = jnp.where(lane < HD_B, qp, 0.0).astype(BF16)
        qm_ref[p, tq:2 * tq] = jnp.where(lane >= HD_B, qp, 0.0).astype(BF16)

    row = lax.broadcasted_iota(jnp.int32, (2 * tq, tq), 0)
    col = lax.broadcasted_iota(jnp.int32, (2 * tq, tq), 1)
    causal = col < jnp.where(row < tq, row, row - tq)
    first = lax.broadcasted_iota(jnp.int32, (2 * tq, 1), 0) < tq
    bias = [jnp.where(first, bias_ref[2 * p], bias_ref[2 * p + 1]) for p in range(n_pairs)]
    pairs = range(n_pairs)

    def sweep(j, diag):
        z = [_dot(qm_ref[p], kt_ref[j, p * LANES:(p + 1) * LANES, :]) + bias[p] for p in pairs]
        lk = [_neg_softplus(z[p]) for p in pairs]
        if diag:
            lk = [jnp.where(causal, lk[p], 0.0) for p in pairs]
        within = [_dot(lk[p].astype(BF16), u_ref[...]) for p in pairs]
        for p in pairs:
            before = jnp.zeros((2 * tq, 1), F32) if diag else carry_ref[p][:, 0:1]
            att = jnp.exp(z[p] + lk[p] + within[p] + before)
            if diag:
                att = jnp.where(causal, att, 0.0)
            carry_ref[p] = jnp.broadcast_to(before + jnp.sum(lk[p], axis=-1, keepdims=True), (2 * tq, LANES))
            pv = _dot(att.astype(BF16), vb_ref[j, p])
            acc_ref[p] = pv if diag else acc_ref[p] + pv

    sweep(i, True)

    def off_diag(t, c):
        sweep(i - 1 - t, False)
        return c

    lax.fori_loop(0, i, off_diag, 0)

    for p in pairs:
        o_ref[:, p * LANES:(p + 1) * LANES] = jnp.where(lane < HD_B, acc_ref[p, 0:tq], acc_ref[p, tq:2 * tq]).astype(
            o_ref.dtype)


def _sb_prompt(u, t, sb_bias):
    tq = SB_BLOCK
    nq = t // tq
    cb = RWKV_COLS // U_COL_BLOCK
    upper = (lax.broadcasted_iota(jnp.int32, (tq, tq), 0) > lax.broadcasted_iota(jnp.int32, (tq, tq), 1)).astype(BF16)
    uspec = lambda c: pl.BlockSpec((tq, U_COL_BLOCK), functools.partial(lambda i, c: (i, c), c=c))
    return pl.pallas_call(
        _sb_prompt_kernel,
        out_shape=jax.ShapeDtypeStruct((t, D_B), BF16),
        grid=(nq,),
        in_specs=[pl.BlockSpec(memory_space=pltpu.SMEM)] + [uspec(cb + c) for c in range(6)]
                 + [pl.BlockSpec((tq, tq), lambda i: (0, 0))],
        out_specs=pl.BlockSpec((tq, D_B), lambda i: (i, 0)),
        scratch_shapes=[pltpu.VMEM((nq, D_B, tq), BF16), pltpu.VMEM((nq, H_B // 2, tq, LANES), BF16),
                        pltpu.VMEM((H_B // 2, 2 * tq, LANES), BF16), pltpu.VMEM((H_B // 2, 2 * tq, LANES), F32),
                        pltpu.VMEM((H_B // 2, 2 * tq, LANES), F32)],
        compiler_params=_params(("arbitrary",)),
        name="sb_prompt",
    )(sb_bias, u, u, u, u, u, u, upper)


Q_PAD = 8


def _sb_sample_kernel(pt_ref, bias_ref, qkv_ref, u_ref, *refs, n_pages, steps):
    k_refs = refs[:n_pages]
    v_refs = refs[n_pages:2 * n_pages]
    o_ref = refs[2 * n_pages]
    scale = HD_B ** -0.5
    rows = H_B * Q_PAD
    x = qkv_ref[0]
    q8 = x[:, 0:D_B] * scale
    zpad = jnp.zeros((PAGE_SIZE - Q_PAD, D_B), F32)
    k_new = jnp.concatenate([x[:, D_B:2 * D_B], zpad], axis=0).astype(BF16)
    v_new = jnp.concatenate([x[:, 2 * D_B:3 * D_B], zpad], axis=0).astype(BF16)
    lane_head = lax.broadcasted_iota(jnp.int32, (Q_PAD, D_B), 1) // HD_B
    qbd = jnp.concatenate([jnp.where(lane_head == h, q8, 0.0) for h in range(H_B)], axis=0).astype(BF16)
    bias = jnp.concatenate([jnp.full((Q_PAD, 1), bias_ref[h], F32) for h in range(H_B)], axis=0)

    z = jnp.concatenate([_dot_nt(qbd, k_refs[p][...].astype(BF16)) for p in range(n_pages)], axis=1) + bias
    zn = _dot_nt(qbd, k_new) + bias
    t_q = lax.broadcasted_iota(jnp.int32, (rows, PAGE_SIZE), 0) % Q_PAD
    c_k = lax.broadcasted_iota(jnp.int32, (rows, PAGE_SIZE), 1)
    new_ok = (c_k < t_q) & (c_k < steps)

    lk = _neg_softplus(z)
    lkn_raw = _neg_softplus(zn)
    lkn = jnp.where(new_ok, lkn_raw, 0.0)
    upper = u_ref[...]
    later_n = _dot(lkn.astype(BF16), upper[:PAGE_SIZE, :PAGE_SIZE])
    att_n = jnp.where(new_ok, jnp.exp(zn + lkn_raw + later_n), 0.0)
    carry = jnp.sum(lkn, axis=-1, keepdims=True)
    blk = upper.shape[0]
    n_blk = n_pages * PAGE_SIZE // blk
    laters = [None] * n_blk
    for b in reversed(range(n_blk)):
        lkb = lk[:, b * blk:(b + 1) * blk]
        laters[b] = _dot(lkb.astype(BF16), upper) + carry
        carry = carry + jnp.sum(lkb, axis=-1, keepdims=True)
    att = jnp.exp(z + lk + jnp.concatenate(laters, axis=1)).astype(BF16)

    out = _dot(att_n.astype(BF16), v_new)
    for p in range(n_pages):
        out = out + _dot(att[:, p * PAGE_SIZE:(p + 1) * PAGE_SIZE], v_refs[p][...].astype(BF16))
    y = jnp.zeros((Q_PAD, D_B), F32)
    for h in range(H_B):
        y = y + jnp.where(lane_head == h, out[h * Q_PAD:(h + 1) * Q_PAD], 0.0)
    o_ref[0] = y


def _sb_sample(qkv_pad, cache_k, cache_v, page_table, sb_bias, layer, steps):
    batch, n_pages = page_table.shape
    blk = MXU_DIM
    upper = (lax.broadcasted_iota(jnp.int32, (blk, blk), 0) > lax.broadcasted_iota(jnp.int32, (blk, blk), 1)).astype(BF16)
    page = lambda p: pl.BlockSpec((None, None, PAGE_SIZE, D_B),
                                  functools.partial(lambda b, pt, p: (layer, pt[b, p], 0, 0), p=p))
    grid_spec = pltpu.PrefetchScalarGridSpec(
        num_scalar_prefetch=1,
        grid=(batch,),
        in_specs=[pl.BlockSpec(memory_space=pltpu.SMEM),
                  pl.BlockSpec((1, Q_PAD, 3 * D_B), lambda b, pt: (b, 0, 0)),
                  pl.BlockSpec((blk, blk), lambda b, pt: (0, 0))]
                 + [page(p) for p in range(n_pages)] * 2,
        out_specs=pl.BlockSpec((1, Q_PAD, D_B), lambda b, pt: (b, 0, 0)),
    )
    return pl.pallas_call(
        functools.partial(_sb_sample_kernel, n_pages=n_pages, steps=steps),
        out_shape=jax.ShapeDtypeStruct((batch, Q_PAD, D_B), F32),
        grid_spec=grid_spec,
        compiler_params=_params(("parallel",)),
        name="sb_sample",
    )(page_table, sb_bias, qkv_pad, upper, *([cache_k] * n_pages), *([cache_v] * n_pages))


def _pool_mix(sums, x, cnts, wp_ref, scale_ref):
    outs = []
    for g in range(len(POOL_WINDOWS)):
        pooled = sums[g] / cnts[g] - x[:, g * C_POOL:(g + 1) * C_POOL]
        outs.append(_dot(pooled.astype(BF16), wp_ref[g].astype(BF16)))
    return jnp.concatenate(outs, axis=1) * scale_ref[...]


def _pool_prompt_kernel(u0_ref, u1_ref, wp_ref, scale_ref, o_ref, ext_ref):
    tp = u0_ref.shape[0]
    pad = 16
    i = pl.program_id(0)

    @pl.when(i == 0)
    def _():
        ext_ref[0:pad, :] = jnp.zeros((pad, D_C), F32)

    x = jnp.concatenate([u0_ref[...], u1_ref[...]], axis=1)
    ext_ref[pad:pad + tp, :] = x
    pos = i * tp + lax.broadcasted_iota(jnp.int32, (tp, 1), 0)
    sums, cnts = [], []
    for g, w in enumerate(POOL_WINDOWS):
        s = x[:, g * C_POOL:(g + 1) * C_POOL]
        for back in range(1, w):
            s = s + ext_ref[pad - back:pad - back + tp, g * C_POOL:(g + 1) * C_POOL]
        sums.append(s)
        cnts.append(jnp.minimum(w, pos + 1).astype(F32))
    o_ref[...] = _pool_mix(sums, x, cnts, wp_ref, scale_ref).astype(o_ref.dtype)
    ext_ref[0:pad, :] = x[tp - pad:, :]


def _pool_prompt(u, t, w_pool, pool_scale, tp):
    cb = (RWKV_COLS + 3 * D_B) // U_COL_BLOCK
    uspec = lambda c: pl.BlockSpec((tp, U_COL_BLOCK), functools.partial(lambda i, c: (i, c), c=c))
    return pl.pallas_call(
        _pool_prompt_kernel,
        out_shape=jax.ShapeDtypeStruct((t, D_C), BF16),
        grid=(t // tp,),
        in_specs=[uspec(cb), uspec(cb + 1),
                  pl.BlockSpec((len(POOL_WINDOWS), C_POOL, C_POOL), lambda i: (0, 0, 0)),
                  pl.BlockSpec((1, D_C), lambda i: (0, 0))],
        out_specs=pl.BlockSpec((tp, D_C), lambda i: (i, 0)),
        scratch_shapes=[pltpu.VMEM((16 + tp, D_C), F32)],
        compiler_params=_params(("arbitrary",)),
        name="pool_prompt",
    )(u, u, w_pool, pool_scale.reshape(1, D_C))


def _pool_sample_kernel(u0_ref, u1_ref, pre_ref, wp_ref, scale_ref, o_ref, *, steps, batch, start_pos):
    x_all = jnp.concatenate([u0_ref[...], u1_ref[...]], axis=1)
    ext = [pre_ref[r] for r in range(POOL_BUF)] + [x_all[t * batch:(t + 1) * batch] for t in range(steps)]
    for t in range(steps):
        sums, cnts = [], []
        for g, w in enumerate(POOL_WINDOWS):
            s = ext[POOL_BUF + t][:, g * C_POOL:(g + 1) * C_POOL]
            for back in range(1, w):
                s = s + ext[POOL_BUF + t - back][:, g * C_POOL:(g + 1) * C_POOL]
            sums.append(s)
            cnts.append(float(min(w, start_pos + t + 1)))
        o_ref[t * batch:(t + 1) * batch, :] = _pool_mix(sums, ext[POOL_BUF + t], cnts, wp_ref, scale_ref).astype(
            o_ref.dtype)


def _pool_sample(u, row_block0, prefix_tm, w_pool, pool_scale, steps, batch, start_pos):
    rows = steps * batch
    cb = (RWKV_COLS + 3 * D_B) // U_COL_BLOCK
    uspec = lambda c: pl.BlockSpec((rows, U_COL_BLOCK), functools.partial(lambda i, c: (row_block0, c), c=c))
    return pl.pallas_call(
        functools.partial(_pool_sample_kernel, steps=steps, batch=batch, start_pos=start_pos),
        out_shape=jax.ShapeDtypeStruct((rows, D_C), BF16),
        grid=(1,),
        in_specs=[uspec(cb), uspec(cb + 1),
                  pl.BlockSpec((POOL_BUF, batch, D_C), lambda i: (0, 0, 0)),
                  pl.BlockSpec((len(POOL_WINDOWS), C_POOL, C_POOL), lambda i: (0, 0, 0)),
                  pl.BlockSpec((1, D_C), lambda i: (0, 0))],
        out_specs=pl.BlockSpec((rows, D_C), lambda i: (0, 0)),
        compiler_params=_params(("arbitrary",)),
        name="pool_sample",
    )(u, u, prefix_tm, w_pool, pool_scale.reshape(1, D_C))


TM_DENSE = 1088
TM_DOWN = 544


def kernel(x_prompt, x_sample, p_prompt, p_sample, cache_k, cache_v, page_table, state_wkv, state_shift, state_pool, g_mix, w_in, mu_shift, w0, w_lora_up, a0, a_lora_up, g_lora_up, k_k, k_a, r_k, ln_w, ln_b, sb_bias, w_pool, pool_scale, w_o, g_ffn, w_gate, w_up, w_down, g_ple, w_pg, w_pp, g_final):
    depth = w_in.shape[0]
    tp = x_prompt.shape[1]
    bs, ts = x_sample.shape[0], x_sample.shape[1]
    rows_s = bs * ts
    m = tp + rows_s
    n_pages = page_table.shape[1]
    past_len = n_pages * PAGE_SIZE

    to_tm = lambda x: jnp.swapaxes(x, 0, 1).reshape(rows_s, x.shape[-1])
    from_tm = lambda x: jnp.swapaxes(x.reshape(ts, bs, x.shape[-1]), 0, 1)

    h = jnp.concatenate([x_prompt[0], to_tm(x_sample)], axis=0)
    cache_k4 = cache_k.reshape(depth, -1, PAGE_SIZE, D_B)
    cache_v4 = cache_v.reshape(depth, -1, PAGE_SIZE, D_B)
    state_flat = state_wkv.reshape(depth, bs, H_A * HD_A * HD_A)

    outs = {n: [] for n in ("kp", "vp", "ks", "vs", "wp", "ws", "sp", "ss", "pp", "ps")}
    for i in range(depth):
        lw = dict(mu_shift=mu_shift[i], w0=w0[i], w_lora_up=w_lora_up[i], a0=a0[i], a_lora_up=a_lora_up[i],
                  g_lora_up=g_lora_up[i], k_k=k_k[i], k_a=k_a[i], r_k=r_k[i].reshape(-1), ln_w=ln_w[i], ln_b=ln_b[i])
        xn = _rmsnorm(h, g_mix[i], BF16, TM_DENSE)
        u = _dense([(xn, w_in[i], D_MODEL, 0, 0)], 1, _epi_plain, None, TM_DENSE, 768, F32, "dense_in")

        prep_p = _rwkv_prep(u, 0, tp, jnp.zeros((1, RWKV_COLS), F32), 1, 256, lw)
        y_p, wkv_p = _rwkv_scan_prompt(*prep_p[:6])
        ya_p = _rwkv_post(y_p, prep_p[7], prep_p[6], lw, 512)
        prep_s = _rwkv_prep(u, tp // rows_s, rows_s, state_shift[i], bs, rows_s, lw)
        y_s, wkv_s = _rwkv_scan_sample(*prep_s[:6], state_flat, i, ts, bs)
        ya_s = _rwkv_post(y_s, prep_s[7], prep_s[6], lw, rows_s)

        yb_p = _sb_prompt(u, tp, sb_bias[i])
        qkv_s = from_tm(u[tp:, RWKV_COLS:RWKV_COLS + 3 * D_B])
        qkv_pad = jnp.pad(qkv_s, ((0, 0), (0, Q_PAD - ts), (0, 0)))
        yb_s = _sb_sample(qkv_pad, cache_k4, cache_v4, page_table, sb_bias[i], i, ts)
        yb_s = to_tm(yb_s[:, :ts]).astype(BF16)

        yc_p = _pool_prompt(u, tp, w_pool[i], pool_scale[i], 512)
        yc_s = _pool_sample(u, tp // rows_s, jnp.swapaxes(state_pool[i], 0, 1), w_pool[i], pool_scale[i],
                            ts, bs, past_len)

        ya = jnp.concatenate([ya_p, ya_s], axis=0)
        yb = jnp.concatenate([yb_p, yb_s], axis=0)
        yc = jnp.concatenate([yc_p, yc_s], axis=0)
        h = _dense([(ya, w_o[i], D_A, 0, 0), (yb, w_o[i], D_B, D_A // D_B, 0),
                    (yc, w_o[i], D_C, (D_A + D_B) // D_C, 0)], 1, _epi_residual, h, TM_DENSE, 512, F32, "dense_o")

        xn = _rmsnorm(h, g_ffn[i], BF16, TM_DENSE)
        act = _dense([(xn, w_gate[i], D_MODEL, 0, 0), (xn, w_up[i], D_MODEL, 0, 1)], 2, _epi_swiglu, None,
                     TM_DENSE, 512, BF16, "dense_ffn_in")
        h = _dense([(act, w_down[i], D_FF, 0, 0)], 1, _epi_residual, h, TM_DOWN, 512, F32, "dense_ffn_out")

        xn = _rmsnorm(h, g_ple[i], BF16, TM_DENSE)
        p_i = jnp.concatenate([p_prompt[i, 0], to_tm(p_sample[i])], axis=0).astype(BF16)
        h = _dense([(xn, w_pg[i], D_MODEL, 0, 0), (p_i, w_pp[i], D_PLE, 0, 1)], 2, _epi_gated_residual, h,
                   TM_DENSE, 512, F32, "dense_ple")

        kv_p = u[:tp, RWKV_COLS + D_B:RWKV_COLS + 3 * D_B]
        outs["kp"].append(kv_p[:, :D_B].reshape(1, tp, H_B, HD_B))
        outs["vp"].append(kv_p[:, D_B:].reshape(1, tp, H_B, HD_B))
        outs["ks"].append(qkv_s[..., D_B:2 * D_B].reshape(bs, ts, H_B, HD_B))
        outs["vs"].append(qkv_s[..., 2 * D_B:].reshape(bs, ts, H_B, HD_B))
        outs["wp"].append(wkv_p.reshape(1, H_A, HD_A, HD_A))
        outs["ws"].append(wkv_s.reshape(bs, H_A, HD_A, HD_A))
        outs["sp"].append(u[tp - 1:tp, :RWKV_COLS])
        outs["ss"].append(u[m - bs:, :RWKV_COLS])
        uc_s = from_tm(u[tp:, RWKV_COLS + 3 * D_B:])
        outs["pp"].append(u[tp - POOL_BUF:tp, RWKV_COLS + 3 * D_B:][None])
        outs["ps"].append(jnp.concatenate([state_pool[i][:, ts:], uc_s], axis=1))

    y = _rmsnorm(h, g_final, F32, TM_DENSE)
    y_prompt = y[:tp][None]
    y_sample = from_tm(y[tp:])
    stack = lambda n: jnp.stack(outs[n])
    return (y_prompt, y_sample, stack("kp"), stack("vp"), stack("ks"), stack("vs"), stack("wp"), stack("ws"),
            stack("sp"), stack("ss"), stack("pp"), stack("ps"))
```

```python
import functools

import jax
import jax.numpy as jnp
from jax import lax
from jax.experimental import pallas as pl
from jax.experimental.pallas import tpu as pltpu

F32 = jnp.float32
BF16 = jnp.bfloat16

D_MODEL = 2048
D_A = 1024
HD_A = 64
H_A = 16
LORA_W = 64
LORA_A = 64
LORA_G = 128
D_B = 512
HD_B = 64
H_B = 8
D_C = 512
POOL_WINDOWS = (2, 4, 8, 16)
C_POOL = 128
POOL_BUF = 15
RWKV_COLS = 3 * D_A + LORA_W + LORA_A + LORA_G
IN_COLS = RWKV_COLS + 3 * D_B + D_C
D_FF = 5632
D_PLE = 256
PAGE_SIZE = 128
EPS = 1e-6
GN_EPS = 64e-5

LANES = 128
MXU_DIM = 256
RWKV_CHUNK = 64
SB_BLOCK = 256
U_COL_BLOCK = 256
VMEM_LIMIT_MB = 56
NEG_LOG2E = -1.4426950408889634


def _params(sem, vmem_mb=VMEM_LIMIT_MB):
    return pltpu.CompilerParams(dimension_semantics=sem, vmem_limit_bytes=vmem_mb * 2**20)


def _dot(a, b):
    return jnp.dot(a, b, preferred_element_type=F32)


def _dot_nt(a, b):
    return lax.dot_general(a, b, (((1,), (1,)), ((), ())), preferred_element_type=F32)


def _split(x):
    hi = x.astype(BF16)
    lo = (x - hi.astype(F32)).astype(BF16)
    return hi, lo


def _dot_left_exact(m_bf16, x):
    hi, lo = _split(x)
    return _dot(m_bf16, hi) + _dot(m_bf16, lo)


def _dot_right_exact(x, m_bf16):
    hi, lo = _split(x)
    return _dot(hi, m_bf16) + _dot(lo, m_bf16)


def _neg_softplus(z):
    e = jnp.exp2(jnp.abs(z) * NEG_LOG2E)
    return jnp.minimum(-z, 0.0) - jnp.log(1.0 + e)


def _sigmoid(x):
    return 1.0 / (1.0 + jnp.exp(-x))


def _rmsnorm_kernel(h_ref, g_ref, o_ref):
    x = h_ref[...]
    y = x * lax.rsqrt(jnp.mean(x * x, axis=-1, keepdims=True) + EPS)
    o_ref[...] = (y * g_ref[...]).astype(o_ref.dtype)


def _rmsnorm(h, g, out_dtype, tm, row_block0=0, rows=None):
    d = h.shape[1]
    m = h.shape[0] if rows is None else rows
    return pl.pallas_call(
        _rmsnorm_kernel,
        out_shape=jax.ShapeDtypeStruct((m, d), out_dtype),
        grid=(m // tm,),
        in_specs=[pl.BlockSpec((tm, d), lambda i: (i + row_block0, 0)), pl.BlockSpec((1, d), lambda i: (0, 0))],
        out_specs=pl.BlockSpec((tm, d), lambda i: (i, 0)),
        compiler_params=_params(("parallel",)),
        name="rmsnorm",
    )(h, g.reshape(1, d))


def _dense_kernel(*refs, n_terms, acc_of, n_acc, has_res, epilogue):
    a_refs = refs[:n_terms]
    w_refs = refs[n_terms:2 * n_terms]
    pos = 2 * n_terms
    res_ref = refs[pos] if has_res else None
    pos += int(has_res)
    o_ref = refs[pos]
    wbf_refs = refs[pos + 1:]

    @pl.when(pl.program_id(1) == 0)
    def _():
        for t in range(n_terms):
            wbf_refs[t][...] = w_refs[t][...].astype(BF16)

    accs = [None] * n_acc
    for t in range(n_terms):
        d = _dot(a_refs[t][...], wbf_refs[t][...])
        accs[acc_of[t]] = d if accs[acc_of[t]] is None else accs[acc_of[t]] + d
    res = res_ref[...] if has_res else None
    o_ref[...] = epilogue(accs, res).astype(o_ref.dtype)


def _dense(terms, layer, n_acc, epilogue, res, tm, tn, out_dtype, name):
    m = terms[0][0].shape[0]
    n = terms[0][1].shape[2]
    n_terms = len(terms)
    in_specs = [pl.BlockSpec((tm, t[2]), lambda j, i: (i, 0)) for t in terms]
    in_specs += [pl.BlockSpec((None, t[2], tn), functools.partial(lambda j, i, r: (layer, r, j), r=t[3]))
                 for t in terms]
    args = [t[0] for t in terms] + [t[1] for t in terms]
    if res is not None:
        in_specs.append(pl.BlockSpec((tm, tn), lambda j, i: (i, j)))
        args.append(res)
    kern = functools.partial(_dense_kernel, n_terms=n_terms, acc_of=[t[4] for t in terms], n_acc=n_acc,
                             has_res=res is not None, epilogue=epilogue)
    return pl.pallas_call(
        kern,
        out_shape=jax.ShapeDtypeStruct((m, n), out_dtype),
        grid=(n // tn, m // tm),
        in_specs=in_specs,
        out_specs=pl.BlockSpec((tm, tn), lambda j, i: (i, j)),
        scratch_shapes=[pltpu.VMEM((t[2], tn), BF16) for t in terms],
        compiler_params=_params(("arbitrary", "arbitrary")),
        name=name,
    )(*args)


def _epi_plain(accs, res):
    return accs[0]


def _epi_residual(accs, res):
    return res + accs[0]


def _epi_swiglu(accs, res):
    gate, up = accs
    return gate * _sigmoid(gate) * up


def _epi_gated_residual(accs, res):
    return res + _sigmoid(accs[0]) * accs[1]


def _seg_mats():
    row_head = lax.broadcasted_iota(jnp.int32, (D_A, LANES), 0) // HD_A
    col = lax.broadcasted_iota(jnp.int32, (D_A, LANES), 1)
    e1 = (row_head == col).astype(BF16)
    return e1, e1.T


def _segsum(x, e1, e2):
    return _dot_right_exact(_dot_right_exact(x, e1), e2)


def _rwkv_prep_kernel(u_ref, prev_ref, mu_ref, w0_ref, a0_ref, kk_ref, ka_ref, rk_ref,
                      wl_ref, al_ref, gl_ref, e1_ref, e2_ref,
                      r_o, k_o, v_o, kkn_o, kka_o, ld_o, g_o, bonus_o, carry_ref, *, lag):
    rows = u_ref.shape[0]

    @pl.when(pl.program_id(0) == 0)
    def _():
        carry_ref[...] = prev_ref[...]

    u = u_ref[...]
    if lag == 1:
        rolled = pltpu.roll(u, 1, axis=0)
        row = lax.broadcasted_iota(jnp.int32, u.shape, 0)
        shifted = jnp.where(row == 0, carry_ref[...], rolled)
        carry_ref[...] = u[rows - 1:rows, :]
    else:
        shifted = jnp.concatenate([carry_ref[...], u[:rows - lag, :]], axis=0)
        carry_ref[...] = u[rows - lag:, :]
    xs = u + (shifted - u) * mu_ref[...]

    r = xs[:, 0:D_A]
    k = xs[:, D_A:2 * D_A]
    v = xs[:, 2 * D_A:3 * D_A]
    x_wa = xs[:, 3 * D_A:3 * D_A + LORA_W + LORA_A]
    xg = xs[:, 3 * D_A + LORA_W + LORA_A:]
    lane = lax.broadcasted_iota(jnp.int32, x_wa.shape, 1)
    tw = jnp.where(lane < LORA_W, jnp.tanh(x_wa), 0.0).astype(BF16)
    xa = jnp.where(lane >= LORA_W, x_wa, 0.0).astype(BF16)
    w_lin = w0_ref[...] + _dot(tw, wl_ref[...].astype(BF16))
    w = _neg_softplus(-w_lin) - 0.5
    ld_o[...] = -jnp.exp(w)
    a = _sigmoid(a0_ref[...] + _dot(xa, al_ref[...].astype(BF16)))
    g_o[...] = _dot(_sigmoid(xg).astype(BF16), gl_ref[...].astype(BF16))

    e1 = e1_ref[...]
    e2 = e2_ref[...]
    kk = k * kk_ref[...]
    norm = jnp.maximum(jnp.sqrt(_segsum(kk * kk, e1, e2)), 1e-12)
    kkn = kk / norm
    k2 = k * (1.0 + (a - 1.0) * ka_ref[...])
    r_o[...] = r
    k_o[...] = k2
    v_o[...] = v
    kkn_o[...] = kkn
    kka_o[...] = kkn * a
    bonus_o[...] = _segsum(r * k2 * rk_ref[...], e1, e2) * v


def _rwkv_prep(u, row_block0, rows, prev, lag, tr, lw):
    e1, e2 = _seg_mats()
    zeros_w = jnp.zeros((LORA_W, D_A), F32)
    wl = jnp.concatenate([lw["w_lora_up"], zeros_w], axis=0)
    al = jnp.concatenate([zeros_w, lw["a_lora_up"]], axis=0)
    vec = lambda x: x.reshape(1, -1)
    const = lambda shape: pl.BlockSpec(shape, lambda i: (0, 0))
    out = jax.ShapeDtypeStruct((rows, D_A), F32)
    ospec = pl.BlockSpec((tr, D_A), lambda i: (i, 0))
    return pl.pallas_call(
        functools.partial(_rwkv_prep_kernel, lag=lag),
        out_shape=[out] * 8,
        grid=(rows // tr,),
        in_specs=[pl.BlockSpec((tr, RWKV_COLS), lambda i: (i + row_block0, 0)),
                  const((lag, RWKV_COLS)), const((1, RWKV_COLS)),
                  const((1, D_A)), const((1, D_A)), const((1, D_A)), const((1, D_A)), const((1, D_A)),
                  const((LORA_W + LORA_A, D_A)), const((LORA_W + LORA_A, D_A)), const((LORA_G, D_A)),
                  const((D_A, LANES)), const((LANES, D_A))],
        out_specs=[ospec] * 8,
        scratch_shapes=[pltpu.VMEM((lag, RWKV_COLS), F32)],
        compiler_params=_params(("arbitrary",)),
        name="rwkv_prep",
    )(u, prev, vec(lw["mu_shift"]), vec(lw["w0"]), vec(lw["a0"]), vec(lw["k_k"]), vec(lw["k_a"]),
      vec(lw["r_k"]), wl, al, lw["g_lora_up"], e1, e2)


def _rwkv_post_kernel(y_ref, bonus_ref, g_ref, lnw_ref, lnb_ref, e1_ref, e2_ref, o_ref):
    e1 = e1_ref[...]
    e2 = e2_ref[...]
    y = y_ref[...]
    mean = _segsum(y, e1, e2) * (1.0 / HD_A)
    d = y - mean
    var = _segsum(d * d, e1, e2) * (1.0 / HD_A)
    yn = d * lax.rsqrt(var + GN_EPS) * lnw_ref[...] + lnb_ref[...]
    o_ref[...] = ((yn + bonus_ref[...]) * g_ref[...]).astype(o_ref.dtype)


def _rwkv_post(y, bonus, g, lw, tr):
    rows = y.shape[0]
    e1, e2 = _seg_mats()
    rspec = pl.BlockSpec((tr, D_A), lambda i: (i, 0))
    const = lambda shape: pl.BlockSpec(shape, lambda i: (0, 0))
    return pl.pallas_call(
        _rwkv_post_kernel,
        out_shape=jax.ShapeDtypeStruct((rows, D_A), BF16),
        grid=(rows // tr,),
        in_specs=[rspec, rspec, rspec, const((1, D_A)), const((1, D_A)), const((D_A, LANES)), const((LANES, D_A))],
        out_specs=rspec,
        compiler_params=_params(("parallel",)),
        name="rwkv_post",
    )(y, bonus, g, lw["ln_w"].reshape(1, D_A), lw["ln_b"].reshape(1, D_A), e1, e2)


def _unit_lower_inverses(ls, n):
    row = lax.broadcasted_iota(jnp.int32, (n, n), 0)
    col = lax.broadcasted_iota(jnp.int32, (n, n), 1)
    eye = (row == col).astype(F32)
    ts = [eye + l for l in ls]
    ps = [l.astype(BF16) for l in ls]
    n_factors = n.bit_length() - 1
    for stage in range(1, n_factors):
        ps = [_dot(p, p).astype(BF16) for p in ps]
        ts = [t + _dot(t.astype(BF16), p) for t, p in zip(ts, ps)]
    return ts


def _rwkv_scan_kernel(r_ref, k_ref, v_ref, kkn_ref, kka_ref, ld_ref, tri_ref, y_ref, s_ref):
    c = RWKV_CHUNK
    heads = range(H_A)

    @pl.when(pl.program_id(0) == 0)
    def _():
        s_ref[...] = jnp.zeros_like(s_ref)

    ld = ld_ref[...]
    cum = _dot_left_exact(tri_ref[...], ld)
    last = cum[c - 1:c, :]
    p_inc = jnp.exp(cum)
    p_exc = jnp.exp(cum - ld)
    p_inv = jnp.exp(-cum)
    p_end = jnp.exp(last - cum)
    p_tot = jnp.exp(last)
    kkn = kkn_ref[...]
    kka = kka_ref[...]
    k2 = k_ref[...]
    a_t = -kkn * p_exc
    r_t = r_ref[...] * p_inc
    b_t = kka * p_inv
    k_t = k2 * p_inv
    b_e = kka * p_end
    k_e = k2 * p_end
    v = v_ref[...]

    row = lax.broadcasted_iota(jnp.int32, (2 * c, 2 * c), 0)
    col = lax.broadcasted_iota(jnp.int32, (2 * c, 2 * c), 1)
    rr = row % c
    cc = col % c
    keep = cc < rr + jnp.where(row < c, 0, 1)
    zeros_cv = jnp.zeros((c, HD_A), F32)

    sl = [slice(h * HD_A, (h + 1) * HD_A) for h in heads]
    s0 = [s_ref[h] for h in heads]
    ar = [jnp.concatenate([a_t[:, sl[h]], r_t[:, sl[h]]], axis=0).astype(BF16) for h in heads]
    bk = [jnp.concatenate([b_t[:, sl[h]], k_t[:, sl[h]]], axis=0).astype(BF16) for h in heads]
    gm = [jnp.where(keep, _dot_nt(ar[h], bk[h]), 0.0) for h in heads]
    ars = [_dot_nt(ar[h], s0[h].astype(BF16)) for h in heads]
    gmb = [g.astype(BF16) for g in gm]
    vh = [v[:, sl[h]] for h in heads]
    x = [ars[h][:c] + _dot(gmb[h][:c, :], jnp.concatenate([zeros_cv, vh[h]], axis=0).astype(BF16)) for h in heads]
    tinv = _unit_lower_inverses([g[:c, :c] for g in gm], c)
    w = [_dot(tinv[h].astype(BF16), x[h].astype(BF16)) for h in heads]
    wv = [jnp.concatenate([w[h], vh[h]], axis=0) for h in heads]
    ys = [ars[h][c:] + _dot(gmb[h][c:, :], wv[h].astype(BF16)) for h in heads]
    be = [jnp.concatenate([b_e[:, sl[h]], k_e[:, sl[h]]], axis=0).astype(BF16) for h in heads]
    s_new = [s0[h] * p_tot[:, sl[h]] + _dot(wv[h].T.astype(BF16), be[h]) for h in heads]
    for h in heads:
        s_ref[h] = s_new[h]
    y_ref[...] = jnp.concatenate(ys, axis=1)


def _rwkv_scan_prompt(r, k2, v, kkn, kka, ld):
    t = r.shape[0]
    c = RWKV_CHUNK
    tri = (lax.broadcasted_iota(jnp.int32, (c, c), 0) >= lax.broadcasted_iota(jnp.int32, (c, c), 1)).astype(BF16)
    rspec = pl.BlockSpec((c, D_A), lambda i: (i, 0))
    return pl.pallas_call(
        _rwkv_scan_kernel,
        out_shape=[jax.ShapeDtypeStruct((t, D_A), F32), jax.ShapeDtypeStruct((H_A, HD_A, HD_A), F32)],
        grid=(t // c,),
        in_specs=[rspec] * 6 + [pl.BlockSpec((c, c), lambda i: (0, 0))],
        out_specs=[rspec, pl.BlockSpec((H_A, HD_A, HD_A), lambda i: (0, 0, 0))],
        compiler_params=_params(("arbitrary",)),
        name="rwkv_scan_prompt",
    )(r, k2, v, kkn, kka, ld, tri)


V_CHUNK = 8


def _rwkv_scan_sample_kernel(r_ref, k_ref, v_ref, kkn_ref, kka_ref, ld_ref, s_in_ref, y_ref, s_out_ref,
                             st_ref, yt_ref, xt_ref, *, steps, batch):
    n_tiles = 2 * HD_A * HD_A // LANES
    for c in range(n_tiles):
        tile = s_in_ref[:, c * LANES:(c + 1) * LANES].T
        st_ref[c // (n_tiles // 2), (c % (n_tiles // 2)) * 2:(c % (n_tiles // 2)) * 2 + 2] = (
            tile.reshape(2, HD_A, batch))
    for q, ref in enumerate((r_ref, k_ref, v_ref, kkn_ref, kka_ref, ld_ref)):
        for t in range(steps):
            x = ref[t * batch:(t + 1) * batch, :]
            if q == 5:
                x = jnp.exp(x)
            xt_ref[q, t] = x.T

    def chunk(i, carry):
        hh = i // (HD_A // V_CHUNK)
        v0 = pl.multiple_of((i % (HD_A // V_CHUNK)) * V_CHUNK, V_CHUNK)
        base = pl.multiple_of(hh * HD_A, HD_A)
        s = st_ref[hh, pl.ds(v0, V_CHUNK)]
        for t in range(steps):
            rt = xt_ref[0, t, pl.ds(base, HD_A), :]
            kt = xt_ref[1, t, pl.ds(base, HD_A), :]
            vt = xt_ref[2, t, pl.ds(base + v0, V_CHUNK), :]
            kkt = xt_ref[3, t, pl.ds(base, HD_A), :]
            kat = xt_ref[4, t, pl.ds(base, HD_A), :]
            dt = xt_ref[5, t, pl.ds(base, HD_A), :]
            sa = jnp.sum(s * kkt[None], axis=1)
            s = s * dt[None] - sa[:, None, :] * kat[None] + vt[:, None, :] * kt[None]
            yt_ref[t, pl.ds(base + v0, V_CHUNK), :] = jnp.sum(s * rt[None], axis=1)
        st_ref[hh, pl.ds(v0, V_CHUNK)] = s
        return carry

    lax.fori_loop(0, 2 * HD_A // V_CHUNK, chunk, 0)

    for t in range(steps):
        y_ref[t * batch:(t + 1) * batch, :] = yt_ref[t].T
    for c in range(n_tiles):
        hh = c // (n_tiles // 2)
        v0 = (c % (n_tiles // 2)) * 2
        tile = st_ref[hh, v0:v0 + 2].reshape(LANES, batch)
        s_out_ref[:, c * LANES:(c + 1) * LANES] = tile.T


def _rwkv_scan_sample(r, k2, v, kkn, kka, ld, state_all, layer, steps, batch):
    rows = r.shape[0]
    pair = 2 * HD_A * HD_A
    xspec = pl.BlockSpec((rows, LANES), lambda p: (0, p))
    return pl.pallas_call(
        functools.partial(_rwkv_scan_sample_kernel, steps=steps, batch=batch),
        out_shape=[jax.ShapeDtypeStruct((rows, D_A), F32),
                   jax.ShapeDtypeStruct((batch, H_A * HD_A * HD_A), F32)],
        grid=(H_A // 2,),
        in_specs=[xspec] * 6 + [pl.BlockSpec((None, batch, pair), lambda p: (layer, 0, p))],
        out_specs=[xspec, pl.BlockSpec((batch, pair), lambda p: (0, p))],
        scratch_shapes=[pltpu.VMEM((2, HD_A, HD_A, batch), F32),
                        pltpu.VMEM((steps, LANES, batch), F32),
                        pltpu.VMEM((6, steps, LANES, batch), F32)],
        compiler_params=_params(("parallel",)),
        name="rwkv_scan_sample",
    )(r, k2, v, kkn, kka, ld, state_all)


def _sb_prompt_kernel(bias_ref, q0_ref, q1_ref, k0_ref, k1_ref, v0_ref, v1_ref, u_ref, o_ref,
                      kt_ref, vb_ref, qm_ref, acc_ref, carry_ref, before_ref, lb_ref, lkb_ref):
    i = pl.program_id(0)
    tq = SB_BLOCK
    scale = HD_B ** -0.5
    n_pairs = H_B // 2

    k_blk = jnp.concatenate([k0_ref[...], k1_ref[...]], axis=1)
    kt_ref[i] = k_blk.T.astype(BF16)
    v_blk = jnp.concatenate([v0_ref[...], v1_ref[...]], axis=1)
    q_blk = jnp.concatenate([q0_ref[...], q1_ref[...]], axis=1) * scale
    lane = lax.broadcasted_iota(jnp.int32, (tq, LANES), 1)
    for p in range(n_pairs):
        vb_ref[i, p] = v_blk[:, p * LANES:(p + 1) * LANES].astype(BF16)
        qp = q_blk[:, p * LANES:(p + 1) * LANES]
        qm_ref[p, 0:tq] = jnp.where(lane < HD_B, qp, 0.0).astype(BF16)
        qm_ref[p, tq:2 * tq] = jnp.where(lane >= HD_B, qp, 0.0).astype(BF16)

    row = lax.broadcasted_iota(jnp.int32, (2 * tq, tq), 0)
    col = lax.broadcasted_iota(jnp.int32, (2 * tq, tq), 1)
    causal = col < jnp.where(row < tq, row, row - tq)
    first = lax.broadcasted_iota(jnp.int32, (2 * tq, 1), 0) < tq
    bias = [jnp.where(first, bias_ref[2 * p], bias_ref[2 * p + 1]) for p in range(n_pairs)]
    pairs = range(n_pairs)

    def raw_scores(j):
        return [_dot(qm_ref[p], kt_ref[j, p * LANES:(p + 1) * LANES, :]) + bias[p] for p in pairs]

    def scores(z, diag):
        lk = [_neg_softplus(z[p]) for p in pairs]
        for p in pairs:
            lb_ref[p] = z[p] + lk[p]
        if diag:
            lk = [jnp.where(causal, lk[p], 0.0) for p in pairs]
        for p in pairs:
            lkb_ref[p] = lk[p].astype(BF16)
            before = jnp.zeros((2 * tq, LANES), F32) if diag else carry_ref[p]
            before_ref[p] = before
            carry_ref[p] = before + jnp.sum(lk[p], axis=-1, keepdims=True)

    def finish(j, diag):
        within = [_dot(lkb_ref[p], u_ref[...]) for p in pairs]
        for p in pairs:
            att = jnp.exp(lb_ref[p] + within[p] + before_ref[p][:, 0:1])
            if diag:
                att = jnp.where(causal, att, 0.0)
            pv = _dot(att.astype(BF16), vb_ref[j, p])
            acc_ref[p] = pv if diag else acc_ref[p] + pv

    scores(raw_scores(i), True)
    finish(i, True)

    @pl.when(i > 0)
    def _():
        scores(raw_scores(i - 1), False)

        def trip(t, c):
            j = i - 1 - t
            finish(j, False)
            scores(raw_scores(j - 1), False)
            return c

        lax.fori_loop(0, i - 1, trip, 0)
        finish(0, False)

    for p in pairs:
        o_ref[:, p * LANES:(p + 1) * LANES] = jnp.where(lane < HD_B, acc_ref[p, 0:tq], acc_ref[p, tq:2 * tq]).astype(
            o_ref.dtype)


def _sb_prompt(u, t, sb_bias):
    tq = SB_BLOCK
    nq = t // tq
    cb = RWKV_COLS // U_COL_BLOCK
    upper = (lax.broadcasted_iota(jnp.int32, (tq, tq), 0) > lax.broadcasted_iota(jnp.int32, (tq, tq), 1)).astype(BF16)
    uspec = lambda c: pl.BlockSpec((tq, U_COL_BLOCK), functools.partial(lambda i, c: (i, c), c=c))
    return pl.pallas_call(
        _sb_prompt_kernel,
        out_shape=jax.ShapeDtypeStruct((t, D_B), BF16),
        grid=(nq,),
        in_specs=[pl.BlockSpec(memory_space=pltpu.SMEM)] + [uspec(cb + c) for c in range(6)]
                 + [pl.BlockSpec((tq, tq), lambda i: (0, 0))],
        out_specs=pl.BlockSpec((tq, D_B), lambda i: (i, 0)),
        scratch_shapes=[pltpu.VMEM((nq, D_B, tq), BF16), pltpu.VMEM((nq, H_B // 2, tq, LANES), BF16),
                        pltpu.VMEM((H_B // 2, 2 * tq, LANES), BF16), pltpu.VMEM((H_B // 2, 2 * tq, LANES), F32),
                        pltpu.VMEM((H_B // 2, 2 * tq, LANES), F32), pltpu.VMEM((H_B // 2, 2 * tq, LANES), F32),
                        pltpu.VMEM((H_B // 2, 2 * tq, tq), F32), pltpu.VMEM((H_B // 2, 2 * tq, tq), BF16)],
        compiler_params=_params(("arbitrary",)),
        name="sb_prompt",
    )(sb_bias, u, u, u, u, u, u, upper)


Q_PAD = 8
HL = PAGE_SIZE * H_B
BF16_ROWS = 16


def _sb_sample_kernel(pt_ref, bias_ref, q_ref, kn_ref, vn_ref, up_ref, e8_ref, e8t_ref, *refs, n_pages, steps):
    k_refs = refs[:n_pages]
    v_refs = refs[n_pages:2 * n_pages]
    o_ref = refs[2 * n_pages]
    n_blk = n_pages + 1
    rows = -(-n_blk * Q_PAD // BF16_ROWS) * BF16_ROWS
    scale = HD_B ** -0.5
    qall = (q_ref[0].reshape(H_B * Q_PAD, HD_B) * scale).astype(BF16)
    zpad = jnp.zeros((PAGE_SIZE - Q_PAD, H_B, HD_B), F32)
    lane_head = lax.broadcasted_iota(jnp.int32, (Q_PAD, HL), 1) & (H_B - 1)

    def own_head(x):
        out = x[0:Q_PAD]
        for h in range(1, H_B):
            out = jnp.where(lane_head == h, x[h * Q_PAD:(h + 1) * Q_PAD], out)
        return out

    def scores(k_page):
        return own_head(_dot_nt(qall, k_page.reshape(HL, HD_B).astype(BF16)))

    z = [scores(k_refs[p][...]) for p in range(n_pages)]
    z.append(scores(jnp.concatenate([kn_ref[0], zpad], axis=0)))
    if rows > n_blk * Q_PAD:
        z.append(jnp.zeros((rows - n_blk * Q_PAD, HL), F32))
    lane_h = lax.broadcasted_iota(jnp.int32, (1, HL), 1) & (H_B - 1)
    bias = jnp.zeros((1, HL), F32)
    for h in range(H_B):
        bias = jnp.where(lane_h == h, bias_ref[h], bias)
    z = jnp.concatenate(z, axis=0) + bias

    row = lax.broadcasted_iota(jnp.int32, (rows, HL), 0)
    tok = lax.shift_right_logical(lax.broadcasted_iota(jnp.int32, (rows, HL), 1), H_B.bit_length() - 1)
    limit = jnp.where(row < n_pages * Q_PAD, PAGE_SIZE,
                      jnp.where(row < n_blk * Q_PAD, jnp.minimum(row - n_pages * Q_PAD, steps), 0))
    valid = tok < limit
    lk = jnp.where(valid, _neg_softplus(z), 0.0)
    within = _dot(lk.astype(BF16), up_ref[...])
    tot = _dot_right_exact(lk, e8_ref[...])
    run = jnp.zeros((Q_PAD, LANES), F32)
    after = [run] * (rows // Q_PAD)
    for b in reversed(range(n_blk)):
        after[b] = run
        run = run + tot[b * Q_PAD:(b + 1) * Q_PAD]
    later = within + _dot_right_exact(jnp.concatenate(after, axis=0), e8t_ref[...])
    att = jnp.where(valid, jnp.exp(z + lk + later), 0.0)

    sel = (lax.broadcasted_iota(jnp.int32, (H_B * Q_PAD, HL), 1) & (H_B - 1)) == (
        lax.broadcasted_iota(jnp.int32, (H_B * Q_PAD, HL), 0) // Q_PAD)
    out = jnp.zeros((H_B * Q_PAD, HD_B), F32)
    for b in range(n_blk):
        v_page = v_refs[b][...] if b < n_pages else jnp.concatenate([vn_ref[0], zpad], axis=0)
        a = att[b * Q_PAD:(b + 1) * Q_PAD]
        a_heads = jnp.where(sel, jnp.concatenate([a] * H_B, axis=0), 0.0).astype(BF16)
        out = out + _dot(a_heads, v_page.reshape(HL, HD_B).astype(BF16))
    o_ref[0] = out


def _sb_sample(q_hq, k_new, v_new, cache_k, cache_v, page_table, sb_bias, layer, steps):
    batch, n_pages = page_table.shape
    r = lax.broadcasted_iota(jnp.int32, (HL, HL), 0)
    c = lax.broadcasted_iota(jnp.int32, (HL, HL), 1)
    same_head = (r & (H_B - 1)) == (c & (H_B - 1))
    upper = (same_head & (r > c)).astype(BF16)
    e8 = ((lax.broadcasted_iota(jnp.int32, (HL, LANES), 0) & (H_B - 1))
          == lax.broadcasted_iota(jnp.int32, (HL, LANES), 1)).astype(BF16)
    page = lambda p: pl.BlockSpec((None, None, PAGE_SIZE, H_B, HD_B),
                                  functools.partial(lambda b, pt, p: (layer, pt[b, p], 0, 0, 0), p=p))
    tok_spec = pl.BlockSpec((1, Q_PAD, H_B, HD_B), lambda b, pt: (b, 0, 0, 0))
    const = lambda shape: pl.BlockSpec(shape, lambda b, pt: (0, 0))
    grid_spec = pltpu.PrefetchScalarGridSpec(
        num_scalar_prefetch=1,
        grid=(batch,),
        in_specs=[pl.BlockSpec(memory_space=pltpu.SMEM), tok_spec, tok_spec, tok_spec,
                  const((HL, HL)), const((HL, LANES)), const((LANES, HL))]
                 + [page(p) for p in range(n_pages)] * 2,
        out_specs=pl.BlockSpec((1, H_B * Q_PAD, HD_B), lambda b, pt: (b, 0, 0)),
    )
    return pl.pallas_call(
        functools.partial(_sb_sample_kernel, n_pages=n_pages, steps=steps),
        out_shape=jax.ShapeDtypeStruct((batch, H_B * Q_PAD, HD_B), F32),
        grid_spec=grid_spec,
        compiler_params=_params(("parallel",)),
        name="sb_sample",
    )(page_table, sb_bias, q_hq, k_new, v_new, upper, e8, e8.T, *([cache_k] * n_pages), *([cache_v] * n_pages))


def _pool_mix(sums, x, cnts, wp_ref, scale_ref):
    outs = []
    for g in range(len(POOL_WINDOWS)):
        pooled = sums[g] / cnts[g] - x[:, g * C_POOL:(g + 1) * C_POOL]
        outs.append(_dot(pooled.astype(BF16), wp_ref[g].astype(BF16)))
    return jnp.concatenate(outs, axis=1) * scale_ref[...]


def _pool_prompt_kernel(u0_ref, u1_ref, wp_ref, scale_ref, o_ref, ext_ref):
    tp = u0_ref.shape[0]
    pad = 16
    i = pl.program_id(0)

    @pl.when(i == 0)
    def _():
        ext_ref[0:pad, :] = jnp.zeros((pad, D_C), F32)

    x = jnp.concatenate([u0_ref[...], u1_ref[...]], axis=1)
    ext_ref[pad:pad + tp, :] = x
    pos = i * tp + lax.broadcasted_iota(jnp.int32, (tp, 1), 0)
    sums, cnts = [], []
    for g, w in enumerate(POOL_WINDOWS):
        s = x[:, g * C_POOL:(g + 1) * C_POOL]
        for back in range(1, w):
            s = s + ext_ref[pad - back:pad - back + tp, g * C_POOL:(g + 1) * C_POOL]
        sums.append(s)
        cnts.append(jnp.minimum(w, pos + 1).astype(F32))
    o_ref[...] = _pool_mix(sums, x, cnts, wp_ref, scale_ref).astype(o_ref.dtype)
    ext_ref[0:pad, :] = x[tp - pad:, :]


def _pool_prompt(u, t, w_pool, pool_scale, tp):
    cb = (RWKV_COLS + 3 * D_B) // U_COL_BLOCK
    uspec = lambda c: pl.BlockSpec((tp, U_COL_BLOCK), functools.partial(lambda i, c: (i, c), c=c))
    return pl.pallas_call(
        _pool_prompt_kernel,
        out_shape=jax.ShapeDtypeStruct((t, D_C), BF16),
        grid=(t // tp,),
        in_specs=[uspec(cb), uspec(cb + 1),
                  pl.BlockSpec((len(POOL_WINDOWS), C_POOL, C_POOL), lambda i: (0, 0, 0)),
                  pl.BlockSpec((1, D_C), lambda i: (0, 0))],
        out_specs=pl.BlockSpec((tp, D_C), lambda i: (i, 0)),
        scratch_shapes=[pltpu.VMEM((16 + tp, D_C), F32)],
        compiler_params=_params(("arbitrary",)),
        name="pool_prompt",
    )(u, u, w_pool, pool_scale.reshape(1, D_C))


def _pool_sample_kernel(u0_ref, u1_ref, pre_ref, wp_ref, scale_ref, o_ref, *, steps, batch, start_pos):
    x_all = jnp.concatenate([u0_ref[...], u1_ref[...]], axis=1)
    ext = [pre_ref[r] for r in range(POOL_BUF)] + [x_all[t * batch:(t + 1) * batch] for t in range(steps)]
    for t in range(steps):
        sums, cnts = [], []
        for g, w in enumerate(POOL_WINDOWS):
            s = ext[POOL_BUF + t][:, g * C_POOL:(g + 1) * C_POOL]
            for back in range(1, w):
                s = s + ext[POOL_BUF + t - back][:, g * C_POOL:(g + 1) * C_POOL]
            sums.append(s)
            cnts.append(float(min(w, start_pos + t + 1)))
        o_ref[t * batch:(t + 1) * batch, :] = _pool_mix(sums, ext[POOL_BUF + t], cnts, wp_ref, scale_ref).astype(
            o_ref.dtype)


def _pool_sample(u, row_block0, prefix_tm, w_pool, pool_scale, steps, batch, start_pos):
    rows = steps * batch
    cb = (RWKV_COLS + 3 * D_B) // U_COL_BLOCK
    uspec = lambda c: pl.BlockSpec((rows, U_COL_BLOCK), functools.partial(lambda i, c: (row_block0, c), c=c))
    return pl.pallas_call(
        functools.partial(_pool_sample_kernel, steps=steps, batch=batch, start_pos=start_pos),
        out_shape=jax.ShapeDtypeStruct((rows, D_C), BF16),
        grid=(1,),
        in_specs=[uspec(cb), uspec(cb + 1),
                  pl.BlockSpec((POOL_BUF, batch, D_C), lambda i: (0, 0, 0)),
                  pl.BlockSpec((len(POOL_WINDOWS), C_POOL, C_POOL), lambda i: (0, 0, 0)),
                  pl.BlockSpec((1, D_C), lambda i: (0, 0))],
        out_specs=pl.BlockSpec((rows, D_C), lambda i: (0, 0)),
        compiler_params=_params(("arbitrary",)),
        name="pool_sample",
    )(u, u, prefix_tm, w_pool, pool_scale.reshape(1, D_C))


TM_DENSE = 1088
TM_DOWN = 544


def kernel(x_prompt, x_sample, p_prompt, p_sample, cache_k, cache_v, page_table, state_wkv, state_shift, state_pool, g_mix, w_in, mu_shift, w0, w_lora_up, a0, a_lora_up, g_lora_up, k_k, k_a, r_k, ln_w, ln_b, sb_bias, w_pool, pool_scale, w_o, g_ffn, w_gate, w_up, w_down, g_ple, w_pg, w_pp, g_final):
    depth = w_in.shape[0]
    tp = x_prompt.shape[1]
    bs, ts = x_sample.shape[0], x_sample.shape[1]
    rows_s = bs * ts
    m = tp + rows_s
    n_pages = page_table.shape[1]
    past_len = n_pages * PAGE_SIZE

    to_tm = lambda x: jnp.swapaxes(x, 0, 1).reshape(rows_s, x.shape[-1])
    from_tm = lambda x: jnp.swapaxes(x.reshape(ts, bs, x.shape[-1]), 0, 1)

    h = jnp.concatenate([x_prompt[0], to_tm(x_sample)], axis=0)
    state_flat = state_wkv.reshape(depth, bs, H_A * HD_A * HD_A)

    outs = {n: [] for n in ("kp", "vp", "ks", "vs", "wp", "ws", "sp", "ss", "pp", "ps")}
    for i in range(depth):
        lw = dict(mu_shift=mu_shift[i], w0=w0[i], w_lora_up=w_lora_up[i], a0=a0[i], a_lora_up=a_lora_up[i],
                  g_lora_up=g_lora_up[i], k_k=k_k[i], k_a=k_a[i], r_k=r_k[i].reshape(-1), ln_w=ln_w[i], ln_b=ln_b[i])
        xn = _rmsnorm(h, g_mix[i], BF16, TM_DENSE)
        u = _dense([(xn, w_in, D_MODEL, 0, 0)], i, 1, _epi_plain, None, TM_DENSE, 768, F32, "dense_in")

        prep_p = _rwkv_prep(u, 0, tp, jnp.zeros((1, RWKV_COLS), F32), 1, 256, lw)
        y_p, wkv_p = _rwkv_scan_prompt(*prep_p[:6])
        ya_p = _rwkv_post(y_p, prep_p[7], prep_p[6], lw, 512)
        prep_s = _rwkv_prep(u, tp // rows_s, rows_s, state_shift[i], bs, rows_s, lw)
        y_s, wkv_s = _rwkv_scan_sample(*prep_s[:6], state_flat, i, ts, bs)
        ya_s = _rwkv_post(y_s, prep_s[7], prep_s[6], lw, rows_s)

        yb_p = _sb_prompt(u, tp, sb_bias[i])
        qkv_s = from_tm(u[tp:, RWKV_COLS:RWKV_COLS + 3 * D_B])
        qkv_pad = jnp.pad(qkv_s, ((0, 0), (0, Q_PAD - ts), (0, 0)))
        per_head = lambda j: qkv_pad[..., j * D_B:(j + 1) * D_B].reshape(bs, Q_PAD, H_B, HD_B)
        yb_s = _sb_sample(jnp.swapaxes(per_head(0), 1, 2), per_head(1), per_head(2), cache_k, cache_v,
                          page_table, sb_bias[i], i, ts)
        yb_s = jnp.swapaxes(yb_s.reshape(bs, H_B, Q_PAD, HD_B), 1, 2)[:, :ts].reshape(bs, ts, D_B)
        yb_s = to_tm(yb_s).astype(BF16)

        yc_p = _pool_prompt(u, tp, w_pool[i], pool_scale[i], 512)
        yc_s = _pool_sample(u, tp // rows_s, jnp.swapaxes(state_pool[i], 0, 1), w_pool[i], pool_scale[i],
                            ts, bs, past_len)

        ya = jnp.concatenate([ya_p, ya_s], axis=0)
        yb = jnp.concatenate([yb_p, yb_s], axis=0)
        yc = jnp.concatenate([yc_p, yc_s], axis=0)
        h = _dense([(ya, w_o, D_A, 0, 0), (yb, w_o, D_B, D_A // D_B, 0),
                    (yc, w_o, D_C, (D_A + D_B) // D_C, 0)], i, 1, _epi_residual, h, TM_DENSE, 512, F32, "dense_o")

        xn = _rmsnorm(h, g_ffn[i], BF16, TM_DENSE)
        act = _dense([(xn, w_gate, D_MODEL, 0, 0), (xn, w_up, D_MODEL, 0, 1)], i, 2, _epi_swiglu, None,
                     TM_DENSE, 512, BF16, "dense_ffn_in")
        h = _dense([(act, w_down, D_FF, 0, 0)], i, 1, _epi_residual, h, TM_DOWN, 512, F32, "dense_ffn_out")

        xn = _rmsnorm(h, g_ple[i], BF16, TM_DENSE)
        p_i = jnp.concatenate([p_prompt[i, 0], to_tm(p_sample[i])], axis=0).astype(BF16)
        h = _dense([(xn, w_pg, D_MODEL, 0, 0), (p_i, w_pp, D_PLE, 0, 1)], i, 2, _epi_gated_residual, h,
                   TM_DENSE, 512, F32, "dense_ple")

        kv_p = u[:tp, RWKV_COLS + D_B:RWKV_COLS + 3 * D_B]
        outs["kp"].append(kv_p[:, :D_B].reshape(1, tp, H_B, HD_B))
        outs["vp"].append(kv_p[:, D_B:].reshape(1, tp, H_B, HD_B))
        outs["ks"].append(qkv_s[..., D_B:2 * D_B].reshape(bs, ts, H_B, HD_B))
        outs["vs"].append(qkv_s[..., 2 * D_B:].reshape(bs, ts, H_B, HD_B))
        outs["wp"].append(wkv_p.reshape(1, H_A, HD_A, HD_A))
        outs["ws"].append(wkv_s.reshape(bs, H_A, HD_A, HD_A))
        outs["sp"].append(u[tp - 1:tp, :RWKV_COLS])
        outs["ss"].append(u[m - bs:, :RWKV_COLS])
        uc_s = from_tm(u[tp:, RWKV_COLS + 3 * D_B:])
        outs["pp"].append(u[tp - POOL_BUF:tp, RWKV_COLS + 3 * D_B:][None])
        outs["ps"].append(jnp.concatenate([state_pool[i][:, ts:], uc_s], axis=1))

    y_prompt = _rmsnorm(h, g_final, F32, rows_s, 0, tp)[None]
    y_sample = from_tm(_rmsnorm(h, g_final, F32, rows_s, tp // rows_s, rows_s))
    stack = lambda n: jnp.stack(outs[n])
    return (y_prompt, y_sample, stack("kp"), stack("vp"), stack("ks"), stack("vs"), stack("wp"), stack("ws"),
            stack("sp"), stack("ss"), stack("pp"), stack("ps"))
```

```python
import functools

import jax
import jax.numpy as jnp
from jax import lax
from jax.experimental import pallas as pl
from jax.experimental.pallas import tpu as pltpu

F32 = jnp.float32
BF16 = jnp.bfloat16

D_MODEL = 2048
D_A = 1024
HD_A = 64
H_A = 16
LORA_W = 64
LORA_A = 64
LORA_G = 128
D_B = 512
HD_B = 64
H_B = 8
D_C = 512
POOL_WINDOWS = (2, 4, 8, 16)
C_POOL = 128
POOL_BUF = 15
RWKV_COLS = 3 * D_A + LORA_W + LORA_A + LORA_G
IN_COLS = RWKV_COLS + 3 * D_B + D_C
D_FF = 5632
D_PLE = 256
PAGE_SIZE = 128
EPS = 1e-6
GN_EPS = 64e-5

LANES = 128
MXU_DIM = 256
RWKV_CHUNK = 64
SB_BLOCK = 256
U_COL_BLOCK = 256
VMEM_LIMIT_MB = 56
NEG_LOG2E = -1.4426950408889634


def _params(sem, vmem_mb=VMEM_LIMIT_MB):
    return pltpu.CompilerParams(dimension_semantics=sem, vmem_limit_bytes=vmem_mb * 2**20)


def _dot(a, b):
    return jnp.dot(a, b, preferred_element_type=F32)


def _dot_nt(a, b):
    return lax.dot_general(a, b, (((1,), (1,)), ((), ())), preferred_element_type=F32)


def _split(x):
    hi = x.astype(BF16)
    lo = (x - hi.astype(F32)).astype(BF16)
    return hi, lo


def _dot_left_exact(m_bf16, x):
    hi, lo = _split(x)
    return _dot(m_bf16, hi) + _dot(m_bf16, lo)


def _dot_right_exact(x, m_bf16):
    hi, lo = _split(x)
    return _dot(hi, m_bf16) + _dot(lo, m_bf16)


def _neg_softplus(z):
    e = jnp.exp2(jnp.abs(z) * NEG_LOG2E)
    return jnp.minimum(-z, 0.0) - jnp.log(1.0 + e)


def _sigmoid(x):
    return 1.0 / (1.0 + jnp.exp(-x))


def _rmsnorm_kernel(h_ref, g_ref, o_ref):
    x = h_ref[...]
    y = x * lax.rsqrt(jnp.mean(x * x, axis=-1, keepdims=True) + EPS)
    o_ref[...] = (y * g_ref[...]).astype(o_ref.dtype)


def _rmsnorm(h, g, out_dtype, tm, row_block0=0, rows=None):
    d = h.shape[1]
    m = h.shape[0] if rows is None else rows
    return pl.pallas_call(
        _rmsnorm_kernel,
        out_shape=jax.ShapeDtypeStruct((m, d), out_dtype),
        grid=(m // tm,),
        in_specs=[pl.BlockSpec((tm, d), lambda i: (i + row_block0, 0)), pl.BlockSpec((1, d), lambda i: (0, 0))],
        out_specs=pl.BlockSpec((tm, d), lambda i: (i, 0)),
        compiler_params=_params(("parallel",)),
        name="rmsnorm",
    )(h, g.reshape(1, d))


def _dense_kernel(*refs, n_terms, acc_of, n_acc, has_res, epilogue):
    a_refs = refs[:n_terms]
    w_refs = refs[n_terms:2 * n_terms]
    pos = 2 * n_terms
    res_ref = refs[pos] if has_res else None
    pos += int(has_res)
    o_ref = refs[pos]
    wbf_refs = refs[pos + 1:]

    @pl.when(pl.program_id(1) == 0)
    def _():
        for t in range(n_terms):
            wbf_refs[t][...] = w_refs[t][...].astype(BF16)

    accs = [None] * n_acc
    for t in range(n_terms):
        d = _dot(a_refs[t][...], wbf_refs[t][...])
        accs[acc_of[t]] = d if accs[acc_of[t]] is None else accs[acc_of[t]] + d
    res = res_ref[...] if has_res else None
    o_ref[...] = epilogue(accs, res).astype(o_ref.dtype)


def _dense(terms, layer, n_acc, epilogue, res, tm, tn, out_dtype, name):
    m = terms[0][0].shape[0]
    n = terms[0][1].shape[2]
    n_terms = len(terms)
    in_specs = [pl.BlockSpec((tm, t[2]), lambda j, i: (i, 0)) for t in terms]
    in_specs += [pl.BlockSpec((None, t[2], tn), functools.partial(lambda j, i, r: (layer, r, j), r=t[3]))
                 for t in terms]
    args = [t[0] for t in terms] + [t[1] for t in terms]
    if res is not None:
        in_specs.append(pl.BlockSpec((tm, tn), lambda j, i: (i, j)))
        args.append(res)
    kern = functools.partial(_dense_kernel, n_terms=n_terms, acc_of=[t[4] for t in terms], n_acc=n_acc,
                             has_res=res is not None, epilogue=epilogue)
    return pl.pallas_call(
        kern,
        out_shape=jax.ShapeDtypeStruct((m, n), out_dtype),
        grid=(n // tn, m // tm),
        in_specs=in_specs,
        out_specs=pl.BlockSpec((tm, tn), lambda j, i: (i, j)),
        scratch_shapes=[pltpu.VMEM((t[2], tn), BF16) for t in terms],
        compiler_params=_params(("arbitrary", "arbitrary")),
        name=name,
    )(*args)


def _epi_plain(accs, res):
    return accs[0]


def _epi_residual(accs, res):
    return res + accs[0]


def _epi_swiglu(accs, res):
    gate, up = accs
    return gate * _sigmoid(gate) * up


def _epi_gated_residual(accs, res):
    return res + _sigmoid(accs[0]) * accs[1]


def _seg_mats():
    row_head = lax.broadcasted_iota(jnp.int32, (D_A, LANES), 0) // HD_A
    col = lax.broadcasted_iota(jnp.int32, (D_A, LANES), 1)
    e1 = (row_head == col).astype(BF16)
    return e1, e1.T


def _segsum(x, e1, e2):
    return _dot_right_exact(_dot_right_exact(x, e1), e2)


def _rwkv_prep_kernel(u_ref, prev_ref, mu_ref, w0_ref, a0_ref, kk_ref, ka_ref, rk_ref,
                      wl_ref, al_ref, gl_ref, e1_ref, e2_ref,
                      r_o, k_o, v_o, kkn_o, kka_o, ld_o, g_o, bonus_o, carry_ref, *, lag):
    rows = u_ref.shape[0]

    @pl.when(pl.program_id(0) == 0)
    def _():
        carry_ref[...] = prev_ref[...]

    u = u_ref[...]
    if lag == 1:
        rolled = pltpu.roll(u, 1, axis=0)
        row = lax.broadcasted_iota(jnp.int32, u.shape, 0)
        shifted = jnp.where(row == 0, carry_ref[...], rolled)
        carry_ref[...] = u[rows - 1:rows, :]
    else:
        shifted = jnp.concatenate([carry_ref[...], u[:rows - lag, :]], axis=0)
        carry_ref[...] = u[rows - lag:, :]
    xs = u + (shifted - u) * mu_ref[...]

    r = xs[:, 0:D_A]
    k = xs[:, D_A:2 * D_A]
    v = xs[:, 2 * D_A:3 * D_A]
    x_wa = xs[:, 3 * D_A:3 * D_A + LORA_W + LORA_A]
    xg = xs[:, 3 * D_A + LORA_W + LORA_A:]
    lane = lax.broadcasted_iota(jnp.int32, x_wa.shape, 1)
    tw = jnp.where(lane < LORA_W, jnp.tanh(x_wa), 0.0).astype(BF16)
    xa = jnp.where(lane >= LORA_W, x_wa, 0.0).astype(BF16)
    w_lin = w0_ref[...] + _dot(tw, wl_ref[...].astype(BF16))
    w = _neg_softplus(-w_lin) - 0.5
    ld_o[...] = -jnp.exp(w)
    a = _sigmoid(a0_ref[...] + _dot(xa, al_ref[...].astype(BF16)))
    g_o[...] = _dot(_sigmoid(xg).astype(BF16), gl_ref[...].astype(BF16))

    e1 = e1_ref[...]
    e2 = e2_ref[...]
    kk = k * kk_ref[...]
    norm = jnp.maximum(jnp.sqrt(_segsum(kk * kk, e1, e2)), 1e-12)
    kkn = kk / norm
    k2 = k * (1.0 + (a - 1.0) * ka_ref[...])
    r_o[...] = r
    k_o[...] = k2
    v_o[...] = v
    kkn_o[...] = kkn
    kka_o[...] = kkn * a
    bonus_o[...] = _segsum(r * k2 * rk_ref[...], e1, e2) * v


def _rwkv_prep(u, row_block0, rows, prev, lag, tr, lw):
    e1, e2 = _seg_mats()
    zeros_w = jnp.zeros((LORA_W, D_A), F32)
    wl = jnp.concatenate([lw["w_lora_up"], zeros_w], axis=0)
    al = jnp.concatenate([zeros_w, lw["a_lora_up"]], axis=0)
    vec = lambda x: x.reshape(1, -1)
    const = lambda shape: pl.BlockSpec(shape, lambda i: (0, 0))
    out = jax.ShapeDtypeStruct((rows, D_A), F32)
    ospec = pl.BlockSpec((tr, D_A), lambda i: (i, 0))
    return pl.pallas_call(
        functools.partial(_rwkv_prep_kernel, lag=lag),
        out_shape=[out] * 8,
        grid=(rows // tr,),
        in_specs=[pl.BlockSpec((tr, RWKV_COLS), lambda i: (i + row_block0, 0)),
                  const((lag, RWKV_COLS)), const((1, RWKV_COLS)),
                  const((1, D_A)), const((1, D_A)), const((1, D_A)), const((1, D_A)), const((1, D_A)),
                  const((LORA_W + LORA_A, D_A)), const((LORA_W + LORA_A, D_A)), const((LORA_G, D_A)),
                  const((D_A, LANES)), const((LANES, D_A))],
        out_specs=[ospec] * 8,
        scratch_shapes=[pltpu.VMEM((lag, RWKV_COLS), F32)],
        compiler_params=_params(("arbitrary",)),
        name="rwkv_prep",
    )(u, prev, vec(lw["mu_shift"]), vec(lw["w0"]), vec(lw["a0"]), vec(lw["k_k"]), vec(lw["k_a"]),
      vec(lw["r_k"]), wl, al, lw["g_lora_up"], e1, e2)


def _rwkv_post_kernel(y_ref, bonus_ref, g_ref, lnw_ref, lnb_ref, e1_ref, e2_ref, o_ref):
    e1 = e1_ref[...]
    e2 = e2_ref[...]
    y = y_ref[...]
    mean = _segsum(y, e1, e2) * (1.0 / HD_A)
    d = y - mean
    var = _segsum(d * d, e1, e2) * (1.0 / HD_A)
    yn = d * lax.rsqrt(var + GN_EPS) * lnw_ref[...] + lnb_ref[...]
    o_ref[...] = ((yn + bonus_ref[...]) * g_ref[...]).astype(o_ref.dtype)


def _rwkv_post(y, bonus, g, lw, tr):
    rows = y.shape[0]
    e1, e2 = _seg_mats()
    rspec = pl.BlockSpec((tr, D_A), lambda i: (i, 0))
    const = lambda shape: pl.BlockSpec(shape, lambda i: (0, 0))
    return pl.pallas_call(
        _rwkv_post_kernel,
        out_shape=jax.ShapeDtypeStruct((rows, D_A), BF16),
        grid=(rows // tr,),
        in_specs=[rspec, rspec, rspec, const((1, D_A)), const((1, D_A)), const((D_A, LANES)), const((LANES, D_A))],
        out_specs=rspec,
        compiler_params=_params(("parallel",)),
        name="rwkv_post",
    )(y, bonus, g, lw["ln_w"].reshape(1, D_A), lw["ln_b"].reshape(1, D_A), e1, e2)


def _unit_lower_inverses(ls, n):
    row = lax.broadcasted_iota(jnp.int32, (n, n), 0)
    col = lax.broadcasted_iota(jnp.int32, (n, n), 1)
    eye = (row == col).astype(F32)
    ts = [eye + l for l in ls]
    ps = [l.astype(BF16) for l in ls]
    n_factors = n.bit_length() - 1
    for stage in range(1, n_factors):
        ps = [_dot(p, p).astype(BF16) for p in ps]
        ts = [t + _dot(t.astype(BF16), p) for t, p in zip(ts, ps)]
    return ts


def _rwkv_scan_kernel(r_ref, k_ref, v_ref, kkn_ref, kka_ref, ld_ref, tri_ref, y_ref, s_ref):
    c = RWKV_CHUNK
    heads = range(H_A)

    @pl.when(pl.program_id(0) == 0)
    def _():
        s_ref[...] = jnp.zeros_like(s_ref)

    ld = ld_ref[...]
    cum = _dot_left_exact(tri_ref[...], ld)
    last = cum[c - 1:c, :]
    p_inc = jnp.exp(cum)
    p_exc = jnp.exp(cum - ld)
    p_inv = jnp.exp(-cum)
    p_end = jnp.exp(last - cum)
    p_tot = jnp.exp(last)
    kkn = kkn_ref[...]
    kka = kka_ref[...]
    k2 = k_ref[...]
    a_t = -kkn * p_exc
    r_t = r_ref[...] * p_inc
    b_t = kka * p_inv
    k_t = k2 * p_inv
    b_e = kka * p_end
    k_e = k2 * p_end
    v = v_ref[...]

    row = lax.broadcasted_iota(jnp.int32, (2 * c, 2 * c), 0)
    col = lax.broadcasted_iota(jnp.int32, (2 * c, 2 * c), 1)
    rr = row % c
    cc = col % c
    keep = cc < rr + jnp.where(row < c, 0, 1)
    zeros_cv = jnp.zeros((c, HD_A), F32)

    sl = [slice(h * HD_A, (h + 1) * HD_A) for h in heads]
    s0 = [s_ref[h] for h in heads]
    ar = [jnp.concatenate([a_t[:, sl[h]], r_t[:, sl[h]]], axis=0).astype(BF16) for h in heads]
    bk = [jnp.concatenate([b_t[:, sl[h]], k_t[:, sl[h]]], axis=0).astype(BF16) for h in heads]
    gm = [jnp.where(keep, _dot_nt(ar[h], bk[h]), 0.0) for h in heads]
    ars = [_dot_nt(ar[h], s0[h].astype(BF16)) for h in heads]
    gmb = [g.astype(BF16) for g in gm]
    vh = [v[:, sl[h]] for h in heads]
    x = [ars[h][:c] + _dot(gmb[h][:c, :], jnp.concatenate([zeros_cv, vh[h]], axis=0).astype(BF16)) for h in heads]
    tinv = _unit_lower_inverses([g[:c, :c] for g in gm], c)
    w = [_dot(tinv[h].astype(BF16), x[h].astype(BF16)) for h in heads]
    wv = [jnp.concatenate([w[h], vh[h]], axis=0) for h in heads]
    ys = [ars[h][c:] + _dot(gmb[h][c:, :], wv[h].astype(BF16)) for h in heads]
    be = [jnp.concatenate([b_e[:, sl[h]], k_e[:, sl[h]]], axis=0).astype(BF16) for h in heads]
    s_new = [s0[h] * p_tot[:, sl[h]] + _dot(wv[h].T.astype(BF16), be[h]) for h in heads]
    for h in heads:
        s_ref[h] = s_new[h]
    y_ref[...] = jnp.concatenate(ys, axis=1)


def _rwkv_scan_prompt(r, k2, v, kkn, kka, ld):
    t = r.shape[0]
    c = RWKV_CHUNK
    tri = (lax.broadcasted_iota(jnp.int32, (c, c), 0) >= lax.broadcasted_iota(jnp.int32, (c, c), 1)).astype(BF16)
    rspec = pl.BlockSpec((c, D_A), lambda i: (i, 0))
    return pl.pallas_call(
        _rwkv_scan_kernel,
        out_shape=[jax.ShapeDtypeStruct((t, D_A), F32), jax.ShapeDtypeStruct((H_A, HD_A, HD_A), F32)],
        grid=(t // c,),
        in_specs=[rspec] * 6 + [pl.BlockSpec((c, c), lambda i: (0, 0))],
        out_specs=[rspec, pl.BlockSpec((H_A, HD_A, HD_A), lambda i: (0, 0, 0))],
        compiler_params=_params(("arbitrary",)),
        name="rwkv_scan_prompt",
    )(r, k2, v, kkn, kka, ld, tri)


V_CHUNK = 8


def _rwkv_scan_sample_kernel(r_ref, k_ref, v_ref, kkn_ref, kka_ref, ld_ref, s_in_ref, y_ref, s_out_ref,
                             yt_ref, xt_ref, *, steps, batch):
    for q, ref in enumerate((r_ref, k_ref, v_ref, kkn_ref, kka_ref, ld_ref)):
        for t in range(steps):
            x = ref[t * batch:(t + 1) * batch, :]
            if q == 5:
                x = jnp.exp(x)
            xt_ref[q, t] = x.T

    def chunk(i, carry):
        hh = i // (HD_A // V_CHUNK)
        v0 = pl.multiple_of((i % (HD_A // V_CHUNK)) * V_CHUNK, V_CHUNK)
        base = pl.multiple_of(hh * HD_A, HD_A)
        s = s_in_ref[hh, pl.ds(v0, V_CHUNK)]
        for t in range(steps):
            rt = xt_ref[0, t, pl.ds(base, HD_A), :]
            kt = xt_ref[1, t, pl.ds(base, HD_A), :]
            vt = xt_ref[2, t, pl.ds(base + v0, V_CHUNK), :]
            kkt = xt_ref[3, t, pl.ds(base, HD_A), :]
            kat = xt_ref[4, t, pl.ds(base, HD_A), :]
            dt = xt_ref[5, t, pl.ds(base, HD_A), :]
            sa = jnp.sum(s * kkt[None], axis=1)
            s = s * dt[None] - sa[:, None, :] * kat[None] + vt[:, None, :] * kt[None]
            yt_ref[t, pl.ds(base + v0, V_CHUNK), :] = jnp.sum(s * rt[None], axis=1)
        s_out_ref[hh, pl.ds(v0, V_CHUNK)] = s
        return carry

    lax.fori_loop(0, 2 * HD_A // V_CHUNK, chunk, 0)

    for t in range(steps):
        y_ref[t * batch:(t + 1) * batch, :] = yt_ref[t].T


def _rwkv_scan_sample(r, k2, v, kkn, kka, ld, state_t, layer, steps, batch):
    rows = r.shape[0]
    xspec = pl.BlockSpec((rows, LANES), lambda p: (0, p))
    return pl.pallas_call(
        functools.partial(_rwkv_scan_sample_kernel, steps=steps, batch=batch),
        out_shape=[jax.ShapeDtypeStruct((rows, D_A), F32),
                   jax.ShapeDtypeStruct((H_A, HD_A, HD_A, batch), F32)],
        grid=(H_A // 2,),
        in_specs=[xspec] * 6 + [pl.BlockSpec((None, 2, HD_A, HD_A, batch), lambda p: (layer, p, 0, 0, 0))],
        out_specs=[xspec, pl.BlockSpec((2, HD_A, HD_A, batch), lambda p: (p, 0, 0, 0))],
        scratch_shapes=[pltpu.VMEM((steps, LANES, batch), F32),
                        pltpu.VMEM((6, steps, LANES, batch), F32)],
        compiler_params=_params(("parallel",)),
        name="rwkv_scan_sample",
    )(r, k2, v, kkn, kka, ld, state_t)


def _sb_prompt_kernel(bias_ref, q0_ref, q1_ref, k0_ref, k1_ref, v0_ref, v1_ref, u_ref, o_ref,
                      kt_ref, vb_ref, qm_ref, acc_ref, carry_ref, before_ref, lb_ref, lkb_ref):
    i = pl.program_id(0)
    tq = SB_BLOCK
    scale = HD_B ** -0.5
    n_pairs = H_B // 2

    k_blk = jnp.concatenate([k0_ref[...], k1_ref[...]], axis=1)
    kt_ref[i] = k_blk.T.astype(BF16)
    v_blk = jnp.concatenate([v0_ref[...], v1_ref[...]], axis=1)
    q_blk = jnp.concatenate([q0_ref[...], q1_ref[...]], axis=1) * scale
    lane = lax.broadcasted_iota(jnp.int32, (tq, LANES), 1)
    for p in range(n_pairs):
        vb_ref[i, p] = v_blk[:, p * LANES:(p + 1) * LANES].astype(BF16)
        qp = q_blk[:, p * LANES:(p + 1) * LANES]
        qm_ref[p, 0:tq] = jnp.where(lane < HD_B, qp, 0.0).astype(BF16)
        qm_ref[p, tq:2 * tq] = jnp.where(lane >= HD_B, qp, 0.0).astype(BF16)

    row = lax.broadcasted_iota(jnp.int32, (2 * tq, tq), 0)
    col = lax.broadcasted_iota(jnp.int32, (2 * tq, tq), 1)
    causal = col < jnp.where(row < tq, row, row - tq)
    first = lax.broadcasted_iota(jnp.int32, (2 * tq, 1), 0) < tq
    bias = [jnp.where(first, bias_ref[2 * p], bias_ref[2 * p + 1]) for p in range(n_pairs)]
    pairs = range(n_pairs)

    def raw_scores(j):
        return [_dot(qm_ref[p], kt_ref[j, p * LANES:(p + 1) * LANES, :]) + bias[p] for p in pairs]

    def scores(z, diag):
        lk = [_neg_softplus(z[p]) for p in pairs]
        for p in pairs:
            lb_ref[p] = z[p] + lk[p]
        if diag:
            lk = [jnp.where(causal, lk[p], 0.0) for p in pairs]
        for p in pairs:
            lkb_ref[p] = lk[p].astype(BF16)
            before = jnp.zeros((2 * tq, LANES), F32) if diag else carry_ref[p]
            before_ref[p] = before
            carry_ref[p] = before + jnp.sum(lk[p], axis=-1, keepdims=True)

    def finish(j, diag):
        within = [_dot(lkb_ref[p], u_ref[...]) for p in pairs]
        for p in pairs:
            att = jnp.exp(lb_ref[p] + within[p] + before_ref[p][:, 0:1])
            if diag:
                att = jnp.where(causal, att, 0.0)
            pv = _dot(att.astype(BF16), vb_ref[j, p])
            acc_ref[p] = pv if diag else acc_ref[p] + pv

    scores(raw_scores(i), True)
    finish(i, True)

    @pl.when(i > 0)
    def _():
        scores(raw_scores(i - 1), False)

        def trip(t, c):
            j = i - 1 - t
            finish(j, False)
            scores(raw_scores(j - 1), False)
            return c

        lax.fori_loop(0, i - 1, trip, 0)
        finish(0, False)

    for p in pairs:
        o_ref[:, p * LANES:(p + 1) * LANES] = jnp.where(lane < HD_B, acc_ref[p, 0:tq], acc_ref[p, tq:2 * tq]).astype(
            o_ref.dtype)


def _sb_prompt(u, t, sb_bias):
    tq = SB_BLOCK
    nq = t // tq
    cb = RWKV_COLS // U_COL_BLOCK
    upper = (lax.broadcasted_iota(jnp.int32, (tq, tq), 0) > lax.broadcasted_iota(jnp.int32, (tq, tq), 1)).astype(BF16)
    uspec = lambda c: pl.BlockSpec((tq, U_COL_BLOCK), functools.partial(lambda i, c: (i, c), c=c))
    return pl.pallas_call(
        _sb_prompt_kernel,
        out_shape=jax.ShapeDtypeStruct((t, D_B), BF16),
        grid=(nq,),
        in_specs=[pl.BlockSpec(memory_space=pltpu.SMEM)] + [uspec(cb + c) for c in range(6)]
                 + [pl.BlockSpec((tq, tq), lambda i: (0, 0))],
        out_specs=pl.BlockSpec((tq, D_B), lambda i: (i, 0)),
        scratch_shapes=[pltpu.VMEM((nq, D_B, tq), BF16), pltpu.VMEM((nq, H_B // 2, tq, LANES), BF16),
                        pltpu.VMEM((H_B // 2, 2 * tq, LANES), BF16), pltpu.VMEM((H_B // 2, 2 * tq, LANES), F32),
                        pltpu.VMEM((H_B // 2, 2 * tq, LANES), F32), pltpu.VMEM((H_B // 2, 2 * tq, LANES), F32),
                        pltpu.VMEM((H_B // 2, 2 * tq, tq), F32), pltpu.VMEM((H_B // 2, 2 * tq, tq), BF16)],
        compiler_params=_params(("arbitrary",)),
        name="sb_prompt",
    )(sb_bias, u, u, u, u, u, u, upper)


Q_PAD = 8


def _sb_sample_kernel(pt_ref, bias_ref, qkv_ref, u_ref, *refs, n_pages, steps):
    k_refs = refs[:n_pages]
    v_refs = refs[n_pages:2 * n_pages]
    o_ref = refs[2 * n_pages]
    scale = HD_B ** -0.5
    rows = H_B * Q_PAD
    x = qkv_ref[0]
    q8 = x[:, 0:D_B] * scale
    zpad = jnp.zeros((PAGE_SIZE - Q_PAD, D_B), F32)
    k_new = jnp.concatenate([x[:, D_B:2 * D_B], zpad], axis=0).astype(BF16)
    v_new = jnp.concatenate([x[:, 2 * D_B:3 * D_B], zpad], axis=0).astype(BF16)
    lane_head = lax.broadcasted_iota(jnp.int32, (Q_PAD, D_B), 1) // HD_B
    qbd = jnp.concatenate([jnp.where(lane_head == h, q8, 0.0) for h in range(H_B)], axis=0).astype(BF16)
    bias = jnp.concatenate([jnp.full((Q_PAD, 1), bias_ref[h], F32) for h in range(H_B)], axis=0)

    page2d = lambda ref: ref[...].reshape(D_B, PAGE_SIZE).astype(BF16)
    z = jnp.concatenate([_dot(qbd, page2d(k_refs[p])) for p in range(n_pages)], axis=1) + bias
    zn = _dot_nt(qbd, k_new) + bias
    t_q = lax.broadcasted_iota(jnp.int32, (rows, PAGE_SIZE), 0) % Q_PAD
    c_k = lax.broadcasted_iota(jnp.int32, (rows, PAGE_SIZE), 1)
    new_ok = c_k < jnp.minimum(t_q, steps)

    lk = _neg_softplus(z)
    lkn_raw = _neg_softplus(zn)
    lkn = jnp.where(new_ok, lkn_raw, 0.0)
    upper = u_ref[...]
    later_n = _dot(lkn.astype(BF16), upper[:PAGE_SIZE, :PAGE_SIZE])
    att_n = jnp.where(new_ok, jnp.exp(zn + lkn_raw + later_n), 0.0)
    carry = jnp.sum(lkn, axis=-1, keepdims=True)
    blk = upper.shape[0]
    n_blk = n_pages * PAGE_SIZE // blk
    laters = [None] * n_blk
    for b in reversed(range(n_blk)):
        lkb = lk[:, b * blk:(b + 1) * blk]
        laters[b] = _dot(lkb.astype(BF16), upper) + carry
        carry = carry + jnp.sum(lkb, axis=-1, keepdims=True)
    att = jnp.exp(z + lk + jnp.concatenate(laters, axis=1)).astype(BF16)

    out = _dot(att_n.astype(BF16), v_new)
    for p in range(n_pages):
        out = out + _dot_nt(att[:, p * PAGE_SIZE:(p + 1) * PAGE_SIZE], page2d(v_refs[p]))
    y = jnp.zeros((Q_PAD, D_B), F32)
    for h in range(H_B):
        y = y + jnp.where(lane_head == h, out[h * Q_PAD:(h + 1) * Q_PAD], 0.0)
    o_ref[0] = y


def _sb_sample(qkv_pad, cache_kt, cache_vt, page_table, sb_bias, layer, steps):
    batch, n_pages = page_table.shape
    blk = MXU_DIM
    upper = (lax.broadcasted_iota(jnp.int32, (blk, blk), 0) > lax.broadcasted_iota(jnp.int32, (blk, blk), 1)).astype(BF16)
    page = lambda p: pl.BlockSpec((None, None, H_B, HD_B, PAGE_SIZE),
                                  functools.partial(lambda b, pt, p: (layer, pt[b, p], 0, 0, 0), p=p))
    grid_spec = pltpu.PrefetchScalarGridSpec(
        num_scalar_prefetch=1,
        grid=(batch,),
        in_specs=[pl.BlockSpec(memory_space=pltpu.SMEM),
                  pl.BlockSpec((1, Q_PAD, 3 * D_B), lambda b, pt: (b, 0, 0)),
                  pl.BlockSpec((blk, blk), lambda b, pt: (0, 0))]
                 + [page(p) for p in range(n_pages)] * 2,
        out_specs=pl.BlockSpec((1, Q_PAD, D_B), lambda b, pt: (b, 0, 0)),
    )
    return pl.pallas_call(
        functools.partial(_sb_sample_kernel, n_pages=n_pages, steps=steps),
        out_shape=jax.ShapeDtypeStruct((batch, Q_PAD, D_B), F32),
        grid_spec=grid_spec,
        compiler_params=_params(("parallel",)),
        name="sb_sample",
    )(page_table, sb_bias, qkv_pad, upper, *([cache_kt] * n_pages), *([cache_vt] * n_pages))


def _pool_mix(sums, x, cnts, wp_ref, scale_ref):
    outs = []
    for g in range(len(POOL_WINDOWS)):
        pooled = sums[g] / cnts[g] - x[:, g * C_POOL:(g + 1) * C_POOL]
        outs.append(_dot(pooled.astype(BF16), wp_ref[g].astype(BF16)))
    return jnp.concatenate(outs, axis=1) * scale_ref[...]


def _pool_prompt_kernel(u0_ref, u1_ref, wp_ref, scale_ref, o_ref, ext_ref):
    tp = u0_ref.shape[0]
    pad = 16
    i = pl.program_id(0)

    @pl.when(i == 0)
    def _():
        ext_ref[0:pad, :] = jnp.zeros((pad, D_C), F32)

    x = jnp.concatenate([u0_ref[...], u1_ref[...]], axis=1)
    ext_ref[pad:pad + tp, :] = x
    pos = i * tp + lax.broadcasted_iota(jnp.int32, (tp, 1), 0)
    sums, cnts = [], []
    for g, w in enumerate(POOL_WINDOWS):
        s = x[:, g * C_POOL:(g + 1) * C_POOL]
        for back in range(1, w):
            s = s + ext_ref[pad - back:pad - back + tp, g * C_POOL:(g + 1) * C_POOL]
        sums.append(s)
        cnts.append(jnp.minimum(w, pos + 1).astype(F32))
    o_ref[...] = _pool_mix(sums, x, cnts, wp_ref, scale_ref).astype(o_ref.dtype)
    ext_ref[0:pad, :] = x[tp - pad:, :]


def _pool_prompt(u, t, w_pool, pool_scale, tp):
    cb = (RWKV_COLS + 3 * D_B) // U_COL_BLOCK
    uspec = lambda c: pl.BlockSpec((tp, U_COL_BLOCK), functools.partial(lambda i, c: (i, c), c=c))
    return pl.pallas_call(
        _pool_prompt_kernel,
        out_shape=jax.ShapeDtypeStruct((t, D_C), BF16),
        grid=(t // tp,),
        in_specs=[uspec(cb), uspec(cb + 1),
                  pl.BlockSpec((len(POOL_WINDOWS), C_POOL, C_POOL), lambda i: (0, 0, 0)),
                  pl.BlockSpec((1, D_C), lambda i: (0, 0))],
        out_specs=pl.BlockSpec((tp, D_C), lambda i: (i, 0)),
        scratch_shapes=[pltpu.VMEM((16 + tp, D_C), F32)],
        compiler_params=_params(("arbitrary",)),
        name="pool_prompt",
    )(u, u, w_pool, pool_scale.reshape(1, D_C))


def _pool_sample_kernel(u0_ref, u1_ref, pre_ref, wp_ref, scale_ref, o_ref, *, steps, batch, start_pos):
    x_all = jnp.concatenate([u0_ref[...], u1_ref[...]], axis=1)
    ext = [pre_ref[r] for r in range(POOL_BUF)] + [x_all[t * batch:(t + 1) * batch] for t in range(steps)]
    for t in range(steps):
        sums, cnts = [], []
        for g, w in enumerate(POOL_WINDOWS):
            s = ext[POOL_BUF + t][:, g * C_POOL:(g + 1) * C_POOL]
            for back in range(1, w):
                s = s + ext[POOL_BUF + t - back][:, g * C_POOL:(g + 1) * C_POOL]
            sums.append(s)
            cnts.append(float(min(w, start_pos + t + 1)))
        o_ref[t * batch:(t + 1) * batch, :] = _pool_mix(sums, ext[POOL_BUF + t], cnts, wp_ref, scale_ref).astype(
            o_ref.dtype)


def _pool_sample(u, row_block0, prefix_tm, w_pool, pool_scale, steps, batch, start_pos):
    rows = steps * batch
    cb = (RWKV_COLS + 3 * D_B) // U_COL_BLOCK
    uspec = lambda c: pl.BlockSpec((rows, U_COL_BLOCK), functools.partial(lambda i, c: (row_block0, c), c=c))
    return pl.pallas_call(
        functools.partial(_pool_sample_kernel, steps=steps, batch=batch, start_pos=start_pos),
        out_shape=jax.ShapeDtypeStruct((rows, D_C), BF16),
        grid=(1,),
        in_specs=[uspec(cb), uspec(cb + 1),
                  pl.BlockSpec((POOL_BUF, batch, D_C), lambda i: (0, 0, 0)),
                  pl.BlockSpec((len(POOL_WINDOWS), C_POOL, C_POOL), lambda i: (0, 0, 0)),
                  pl.BlockSpec((1, D_C), lambda i: (0, 0))],
        out_specs=pl.BlockSpec((rows, D_C), lambda i: (0, 0)),
        compiler_params=_params(("arbitrary",)),
        name="pool_sample",
    )(u, u, prefix_tm, w_pool, pool_scale.reshape(1, D_C))


TM_DENSE = 1088
TM_DOWN = 544


def kernel(x_prompt, x_sample, p_prompt, p_sample, cache_k, cache_v, page_table, state_wkv, state_shift, state_pool, g_mix, w_in, mu_shift, w0, w_lora_up, a0, a_lora_up, g_lora_up, k_k, k_a, r_k, ln_w, ln_b, sb_bias, w_pool, pool_scale, w_o, g_ffn, w_gate, w_up, w_down, g_ple, w_pg, w_pp, g_final):
    depth = w_in.shape[0]
    tp = x_prompt.shape[1]
    bs, ts = x_sample.shape[0], x_sample.shape[1]
    rows_s = bs * ts
    m = tp + rows_s
    n_pages = page_table.shape[1]
    past_len = n_pages * PAGE_SIZE

    to_tm = lambda x: jnp.swapaxes(x, 0, 1).reshape(rows_s, x.shape[-1])
    from_tm = lambda x: jnp.swapaxes(x.reshape(ts, bs, x.shape[-1]), 0, 1)

    h = jnp.concatenate([x_prompt[0], to_tm(x_sample)], axis=0)
    cache_kt = jnp.transpose(cache_k, (0, 1, 3, 4, 2))
    cache_vt = jnp.transpose(cache_v, (0, 1, 3, 4, 2))
    state_t = jnp.transpose(state_wkv, (0, 2, 3, 4, 1))

    outs = {n: [] for n in ("kp", "vp", "ks", "vs", "wp", "ws", "sp", "ss", "pp", "ps")}
    for i in range(depth):
        lw = dict(mu_shift=mu_shift[i], w0=w0[i], w_lora_up=w_lora_up[i], a0=a0[i], a_lora_up=a_lora_up[i],
                  g_lora_up=g_lora_up[i], k_k=k_k[i], k_a=k_a[i], r_k=r_k[i].reshape(-1), ln_w=ln_w[i], ln_b=ln_b[i])
        xn = _rmsnorm(h, g_mix[i], BF16, TM_DENSE)
        u = _dense([(xn, w_in, D_MODEL, 0, 0)], i, 1, _epi_plain, None, TM_DENSE, 768, F32, "dense_in")

        prep_p = _rwkv_prep(u, 0, tp, jnp.zeros((1, RWKV_COLS), F32), 1, 256, lw)
        y_p, wkv_p = _rwkv_scan_prompt(*prep_p[:6])
        ya_p = _rwkv_post(y_p, prep_p[7], prep_p[6], lw, 512)
        prep_s = _rwkv_prep(u, tp // rows_s, rows_s, state_shift[i], bs, rows_s, lw)
        y_s, wkv_s = _rwkv_scan_sample(*prep_s[:6], state_t, i, ts, bs)
        ya_s = _rwkv_post(y_s, prep_s[7], prep_s[6], lw, rows_s)

        yb_p = _sb_prompt(u, tp, sb_bias[i])
        qkv_s = from_tm(u[tp:, RWKV_COLS:RWKV_COLS + 3 * D_B])
        qkv_pad = jnp.pad(qkv_s, ((0, 0), (0, Q_PAD - ts), (0, 0)))
        yb_s = _sb_sample(qkv_pad, cache_kt, cache_vt, page_table, sb_bias[i], i, ts)
        yb_s = to_tm(yb_s[:, :ts]).astype(BF16)

        yc_p = _pool_prompt(u, tp, w_pool[i], pool_scale[i], 512)
        yc_s = _pool_sample(u, tp // rows_s, jnp.swapaxes(state_pool[i], 0, 1), w_pool[i], pool_scale[i],
                            ts, bs, past_len)

        ya = jnp.concatenate([ya_p, ya_s], axis=0)
        yb = jnp.concatenate([yb_p, yb_s], axis=0)
        yc = jnp.concatenate([yc_p, yc_s], axis=0)
        h = _dense([(ya, w_o, D_A, 0, 0), (yb, w_o, D_B, D_A // D_B, 0),
                    (yc, w_o, D_C, (D_A + D_B) // D_C, 0)], i, 1, _epi_residual, h, TM_DENSE, 512, F32, "dense_o")

        xn = _rmsnorm(h, g_ffn[i], BF16, TM_DENSE)
        act = _dense([(xn, w_gate, D_MODEL, 0, 0), (xn, w_up, D_MODEL, 0, 1)], i, 2, _epi_swiglu, None,
                     TM_DENSE, 512, BF16, "dense_ffn_in")
        h = _dense([(act, w_down, D_FF, 0, 0)], i, 1, _epi_residual, h, TM_DOWN, 512, F32, "dense_ffn_out")

        xn = _rmsnorm(h, g_ple[i], BF16, TM_DENSE)
        p_i = jnp.concatenate([p_prompt[i, 0], to_tm(p_sample[i])], axis=0).astype(BF16)
        h = _dense([(xn, w_pg, D_MODEL, 0, 0), (p_i, w_pp, D_PLE, 0, 1)], i, 2, _epi_gated_residual, h,
                   TM_DENSE, 512, F32, "dense_ple")

        kv_p = u[:tp, RWKV_COLS + D_B:RWKV_COLS + 3 * D_B]
        outs["kp"].append(kv_p[:, :D_B].reshape(1, tp, H_B, HD_B))
        outs["vp"].append(kv_p[:, D_B:].reshape(1, tp, H_B, HD_B))
        outs["ks"].append(qkv_s[..., D_B:2 * D_B].reshape(bs, ts, H_B, HD_B))
        outs["vs"].append(qkv_s[..., 2 * D_B:].reshape(bs, ts, H_B, HD_B))
        outs["wp"].append(wkv_p.reshape(1, H_A, HD_A, HD_A))
        outs["ws"].append(jnp.transpose(wkv_s, (3, 0, 1, 2)))
        outs["sp"].append(u[tp - 1:tp, :RWKV_COLS])
        outs["ss"].append(u[m - bs:, :RWKV_COLS])
        uc_s = from_tm(u[tp:, RWKV_COLS + 3 * D_B:])
        outs["pp"].append(u[tp - POOL_BUF:tp, RWKV_COLS + 3 * D_B:][None])
        outs["ps"].append(jnp.concatenate([state_pool[i][:, ts:], uc_s], axis=1))

    y_prompt = _rmsnorm(h, g_final, F32, rows_s, 0, tp)[None]
    y_sample = from_tm(_rmsnorm(h, g_final, F32, rows_s, tp // rows_s, rows_s))
    stack = lambda n: jnp.stack(outs[n])
    return (y_prompt, y_sample, stack("kp"), stack("vp"), stack("ks"), stack("vs"), stack("wp"), stack("ws"),
            stack("sp"), stack("ss"), stack("pp"), stack("ps"))
```

```python
import functools

import jax
import jax.numpy as jnp
from jax import lax
from jax.experimental import pallas as pl
from jax.experimental.pallas import tpu as pltpu

F32 = jnp.float32
BF16 = jnp.bfloat16

D_MODEL = 2048
D_A = 1024
HD_A = 64
H_A = 16
LORA_W = 64
LORA_A = 64
LORA_G = 128
D_B = 512
HD_B = 64
H_B = 8
D_C = 512
POOL_WINDOWS = (2, 4, 8, 16)
C_POOL = 128
POOL_BUF = 15
RWKV_COLS = 3 * D_A + LORA_W + LORA_A + LORA_G
IN_COLS = RWKV_COLS + 3 * D_B + D_C
D_FF = 5632
D_PLE = 256
PAGE_SIZE = 128
EPS = 1e-6
GN_EPS = 64e-5

LANES = 128
MXU_DIM = 256
RWKV_CHUNK = 64
RWKV_CHUNKS_PER_STEP = 4
SB_BLOCK = 256
SB_GROUP = 1
U_COL_BLOCK = 256
VMEM_LIMIT_MB = 56
NEG_LOG2E = -1.4426950408889634


def _params(sem, vmem_mb=VMEM_LIMIT_MB):
    return pltpu.CompilerParams(dimension_semantics=sem, vmem_limit_bytes=vmem_mb * 2**20)


def _dot(a, b):
    return jnp.dot(a, b, preferred_element_type=F32)


def _dot_nt(a, b):
    return lax.dot_general(a, b, (((1,), (1,)), ((), ())), preferred_element_type=F32)


def _split(x):
    hi = x.astype(BF16)
    lo = (x - hi.astype(F32)).astype(BF16)
    return hi, lo


def _dot_left_exact(m_bf16, x):
    hi, lo = _split(x)
    return _dot(m_bf16, hi) + _dot(m_bf16, lo)


def _dot_right_exact(x, m_bf16):
    hi, lo = _split(x)
    return _dot(hi, m_bf16) + _dot(lo, m_bf16)


def _neg_softplus(z):
    return _neg_softplus_of_neg(-z)


def _neg_softplus_of_neg(zn):
    e = jnp.exp2(jnp.abs(zn) * NEG_LOG2E)
    return jnp.minimum(zn, 0.0) - jnp.log(1.0 + e)


def _sigmoid(x):
    return 1.0 / (1.0 + jnp.exp(-x))


def _rmsnorm_kernel(h_ref, g_ref, o_ref):
    x = h_ref[...]
    y = x * lax.rsqrt(jnp.mean(x * x, axis=-1, keepdims=True) + EPS)
    o_ref[...] = (y * g_ref[...]).astype(o_ref.dtype)


def _rmsnorm(h, g, out_dtype, tm, row_block0=0, rows=None):
    d = h.shape[1]
    m = h.shape[0] if rows is None else rows
    return pl.pallas_call(
        _rmsnorm_kernel,
        out_shape=jax.ShapeDtypeStruct((m, d), out_dtype),
        grid=(m // tm,),
        in_specs=[pl.BlockSpec((tm, d), lambda i: (i + row_block0, 0)), pl.BlockSpec((1, d), lambda i: (0, 0))],
        out_specs=pl.BlockSpec((tm, d), lambda i: (i, 0)),
        compiler_params=_params(("parallel",)),
        name="rmsnorm",
    )(h, g.reshape(1, d))


def _dense_kernel(*refs, n_terms, a_of, acc_of, n_acc, has_res, epilogue):
    n_a = max(a_of) + 1
    a_refs = [refs[a_of[t]] for t in range(n_terms)]
    w_refs = refs[n_a:n_a + n_terms]
    pos = n_a + n_terms
    res_ref = refs[pos] if has_res else None
    pos += int(has_res)
    o_ref = refs[pos]
    wbf_refs = refs[pos + 1:]

    @pl.when(pl.program_id(1) == 0)
    def _():
        for t in range(n_terms):
            wbf_refs[t][...] = w_refs[t][...].astype(BF16)

    tm = o_ref.shape[0]
    for r0 in range(0, tm, DENSE_SUB_ROWS):
        rows = slice(r0, r0 + DENSE_SUB_ROWS)
        accs = [None] * n_acc
        for t in range(n_terms):
            d = _dot(a_refs[t][rows, :], wbf_refs[t][...])
            accs[acc_of[t]] = d if accs[acc_of[t]] is None else accs[acc_of[t]] + d
        res = res_ref[rows, :] if has_res else None
        o_ref[rows, :] = epilogue(accs, res).astype(o_ref.dtype)


def _dense(terms, layer, n_acc, epilogue, res, tm, tn, out_dtype, name):
    m = terms[0][0].shape[0]
    n = terms[0][1].shape[2]
    n_terms = len(terms)
    acts = []
    a_of = []
    for t in terms:
        if not any(t[0] is a for a in acts):
            acts.append(t[0])
        a_of.append([t[0] is a for a in acts].index(True))
    in_specs = [pl.BlockSpec((tm, a.shape[1]), lambda j, i: (i, 0)) for a in acts]
    in_specs += [pl.BlockSpec((None, t[2], tn), functools.partial(lambda j, i, r: (layer, r, j), r=t[3]))
                 for t in terms]
    args = acts + [t[1] for t in terms]
    if res is not None:
        in_specs.append(pl.BlockSpec((tm, tn), lambda j, i: (i, j)))
        args.append(res)
    kern = functools.partial(_dense_kernel, n_terms=n_terms, a_of=a_of, acc_of=[t[4] for t in terms], n_acc=n_acc,
                             has_res=res is not None, epilogue=epilogue)
    return pl.pallas_call(
        kern,
        out_shape=jax.ShapeDtypeStruct((m, n), out_dtype),
        grid=(n // tn, m // tm),
        in_specs=in_specs,
        out_specs=pl.BlockSpec((tm, tn), lambda j, i: (i, j)),
        scratch_shapes=[pltpu.VMEM((t[2], tn), BF16) for t in terms],
        compiler_params=_params(("arbitrary", "arbitrary")),
        name=name,
    )(*args)


def _epi_plain(accs, res):
    return accs[0]


def _epi_residual(accs, res):
    return res + accs[0]


def _epi_swiglu(accs, res):
    gate, up = accs
    return gate * _sigmoid(gate) * up


def _epi_gated_residual(accs, res):
    return res + _sigmoid(accs[0]) * accs[1]


def _seg_mats():
    row_head = lax.broadcasted_iota(jnp.int32, (D_A, LANES), 0) // HD_A
    col = lax.broadcasted_iota(jnp.int32, (D_A, LANES), 1)
    e1 = (row_head == col).astype(BF16)
    return e1, e1.T


def _segsum(x, e1, e2):
    return _dot_right_exact(_dot_right_exact(x, e1), e2)


def _rwkv_prep_kernel(u_ref, prev_ref, mu_ref, w0_ref, a0_ref, kk_ref, ka_ref, rk_ref,
                      wl_ref, al_ref, gl_ref, e1_ref, e2_ref,
                      r_o, k_o, v_o, kkn_o, kka_o, ld_o, g_o, bonus_o, carry_ref, *, lag):
    rows = u_ref.shape[0]

    @pl.when(pl.program_id(0) == 0)
    def _():
        carry_ref[...] = prev_ref[...]

    u = u_ref[...]
    if lag == 1:
        rolled = pltpu.roll(u, 1, axis=0)
        row = lax.broadcasted_iota(jnp.int32, u.shape, 0)
        shifted = jnp.where(row == 0, carry_ref[...], rolled)
        carry_ref[...] = u[rows - 1:rows, :]
    else:
        shifted = jnp.concatenate([carry_ref[...], u[:rows - lag, :]], axis=0)
        carry_ref[...] = u[rows - lag:, :]
    xs = u + (shifted - u) * mu_ref[...]

    r = xs[:, 0:D_A]
    k = xs[:, D_A:2 * D_A]
    v = xs[:, 2 * D_A:3 * D_A]
    x_wa = xs[:, 3 * D_A:3 * D_A + LORA_W + LORA_A]
    xg = xs[:, 3 * D_A + LORA_W + LORA_A:]
    lane = lax.broadcasted_iota(jnp.int32, x_wa.shape, 1)
    tw = jnp.where(lane < LORA_W, jnp.tanh(x_wa), 0.0).astype(BF16)
    xa = jnp.where(lane >= LORA_W, x_wa, 0.0).astype(BF16)
    w_lin = w0_ref[...] + _dot(tw, wl_ref[...].astype(BF16))
    w = _neg_softplus(-w_lin) - 0.5
    ld_o[...] = -jnp.exp(w)
    a = _sigmoid(a0_ref[...] + _dot(xa, al_ref[...].astype(BF16)))
    g_o[...] = _dot(_sigmoid(xg).astype(BF16), gl_ref[...].astype(BF16))

    e1 = e1_ref[...]
    e2 = e2_ref[...]
    kk = k * kk_ref[...]
    norm = jnp.maximum(jnp.sqrt(_segsum(kk * kk, e1, e2)), 1e-12)
    kkn = kk / norm
    k2 = k * (1.0 + (a - 1.0) * ka_ref[...])
    r_o[...] = r
    k_o[...] = k2
    v_o[...] = v
    kkn_o[...] = kkn
    kka_o[...] = kkn * a
    bonus_o[...] = _segsum(r * k2 * rk_ref[...], e1, e2) * v


def _rwkv_prep(u, row_block0, rows, prev, lag, tr, lw):
    e1, e2 = _seg_mats()
    zeros_w = jnp.zeros((LORA_W, D_A), F32)
    wl = jnp.concatenate([lw["w_lora_up"], zeros_w], axis=0)
    al = jnp.concatenate([zeros_w, lw["a_lora_up"]], axis=0)
    vec = lambda x: x.reshape(1, -1)
    const = lambda shape: pl.BlockSpec(shape, lambda i: (0, 0))
    out = jax.ShapeDtypeStruct((rows, D_A), F32)
    ospec = pl.BlockSpec((tr, D_A), lambda i: (i, 0))
    return pl.pallas_call(
        functools.partial(_rwkv_prep_kernel, lag=lag),
        out_shape=[out] * 8,
        grid=(rows // tr,),
        in_specs=[pl.BlockSpec((tr, RWKV_COLS), lambda i: (i + row_block0, 0)),
                  const((lag, RWKV_COLS)), const((1, RWKV_COLS)),
                  const((1, D_A)), const((1, D_A)), const((1, D_A)), const((1, D_A)), const((1, D_A)),
                  const((LORA_W + LORA_A, D_A)), const((LORA_W + LORA_A, D_A)), const((LORA_G, D_A)),
                  const((D_A, LANES)), const((LANES, D_A))],
        out_specs=[ospec] * 8,
        scratch_shapes=[pltpu.VMEM((lag, RWKV_COLS), F32)],
        compiler_params=_params(("arbitrary",)),
        name="rwkv_prep",
    )(u, prev, vec(lw["mu_shift"]), vec(lw["w0"]), vec(lw["a0"]), vec(lw["k_k"]), vec(lw["k_a"]),
      vec(lw["r_k"]), wl, al, lw["g_lora_up"], e1, e2)


def _rwkv_post_kernel(y_ref, bonus_ref, g_ref, lnw_ref, lnb_ref, e1_ref, e2_ref, o_ref):
    e1 = e1_ref[...]
    e2 = e2_ref[...]
    y = y_ref[...]
    mean = _segsum(y, e1, e2) * (1.0 / HD_A)
    d = y - mean
    var = _segsum(d * d, e1, e2) * (1.0 / HD_A)
    yn = d * lax.rsqrt(var + GN_EPS) * lnw_ref[...] + lnb_ref[...]
    o_ref[...] = ((yn + bonus_ref[...]) * g_ref[...]).astype(o_ref.dtype)


def _rwkv_post(y, bonus, g, lw, tr):
    rows = y.shape[0]
    e1, e2 = _seg_mats()
    rspec = pl.BlockSpec((tr, D_A), lambda i: (i, 0))
    const = lambda shape: pl.BlockSpec(shape, lambda i: (0, 0))
    return pl.pallas_call(
        _rwkv_post_kernel,
        out_shape=jax.ShapeDtypeStruct((rows, D_A), BF16),
        grid=(rows // tr,),
        in_specs=[rspec, rspec, rspec, const((1, D_A)), const((1, D_A)), const((D_A, LANES)), const((LANES, D_A))],
        out_specs=rspec,
        compiler_params=_params(("parallel",)),
        name="rwkv_post",
    )(y, bonus, g, lw["ln_w"].reshape(1, D_A), lw["ln_b"].reshape(1, D_A), e1, e2)


def _unit_lower_inverses(ls, n):
    row = lax.broadcasted_iota(jnp.int32, (n, n), 0)
    col = lax.broadcasted_iota(jnp.int32, (n, n), 1)
    eye = (row == col).astype(F32)
    ts = [eye + l for l in ls]
    ps = [l.astype(BF16) for l in ls]
    n_factors = n.bit_length() - 1
    for stage in range(1, n_factors):
        ps = [_dot(p, p).astype(BF16) for p in ps]
        ts = [t + _dot(t.astype(BF16), p) for t, p in zip(ts, ps)]
    return ts


def _rwkv_scan_kernel(r_ref, k_ref, v_ref, kkn_ref, kka_ref, ld_ref, tri_ref, y_ref, s_ref):
    c = RWKV_CHUNK
    n_sub = RWKV_CHUNKS_PER_STEP
    heads = range(H_A)
    sl = [slice(h * HD_A, (h + 1) * HD_A) for h in heads]

    @pl.when(pl.program_id(0) == 0)
    def _():
        s_ref[...] = jnp.zeros_like(s_ref)

    ld_all = ld_ref[...]
    cum_all = _dot_left_exact(tri_ref[...], ld_all)
    row = lax.broadcasted_iota(jnp.int32, (2 * c, 2 * c), 0)
    col = lax.broadcasted_iota(jnp.int32, (2 * c, 2 * c), 1)
    keep = col % c < row % c + jnp.where(row < c, 0, 1)
    zeros_cv = jnp.zeros((c, HD_A), F32)

    pre = []
    for j in range(n_sub):
        rows = slice(j * c, (j + 1) * c)
        ld, cum = ld_all[rows], cum_all[rows]
        last = cum[c - 1:c, :]
        p_inv = jnp.exp(-cum)
        p_end = jnp.exp(last - cum)
        kkn, kka, k2, v = kkn_ref[rows, :], kka_ref[rows, :], k_ref[rows, :], v_ref[rows, :]
        a_t = -kkn * jnp.exp(cum - ld)
        r_t = r_ref[rows, :] * jnp.exp(cum)
        b_t, k_t, b_e, k_e = kka * p_inv, k2 * p_inv, kka * p_end, k2 * p_end
        ar = [jnp.concatenate([a_t[:, sl[h]], r_t[:, sl[h]]], axis=0).astype(BF16) for h in heads]
        bk = [jnp.concatenate([b_t[:, sl[h]], k_t[:, sl[h]]], axis=0).astype(BF16) for h in heads]
        be = [jnp.concatenate([b_e[:, sl[h]], k_e[:, sl[h]]], axis=0).astype(BF16) for h in heads]
        pre.append(dict(ar=ar, bk=bk, be=be, vh=[v[:, sl[h]] for h in heads], p_tot=jnp.exp(last)))
    for p in pre:
        p["gm"] = [jnp.where(keep, _dot_nt(p["ar"][h], p["bk"][h]), 0.0) for h in heads]
        p["gmb"] = [g.astype(BF16) for g in p["gm"]]
    for p in pre:
        p["lakv"] = [_dot(p["gmb"][h][:c, :], jnp.concatenate([zeros_cv, p["vh"][h]], axis=0).astype(BF16))
                     for h in heads]
    tinv = _unit_lower_inverses([p["gm"][h][:c, :c] for p in pre for h in heads], c)
    tinv = [t.astype(BF16) for t in tinv]

    s = [s_ref[h] for h in heads]
    for j, p in enumerate(pre):
        ars = [_dot_nt(p["ar"][h], s[h].astype(BF16)) for h in heads]
        x = [ars[h][:c] + p["lakv"][h] for h in heads]
        w = [_dot(tinv[j * H_A + h], x[h].astype(BF16)) for h in heads]
        wv = [jnp.concatenate([w[h], p["vh"][h]], axis=0) for h in heads]
        ys = [ars[h][c:] + _dot(p["gmb"][h][c:, :], wv[h].astype(BF16)) for h in heads]
        s = [s[h] * p["p_tot"][:, sl[h]] + _dot(wv[h].T.astype(BF16), p["be"][h]) for h in heads]
        y_ref[j * c:(j + 1) * c, :] = jnp.concatenate(ys, axis=1)
    for h in heads:
        s_ref[h] = s[h]


def _rwkv_scan_prompt(r, k2, v, kkn, kka, ld):
    t = r.shape[0]
    c = RWKV_CHUNK * RWKV_CHUNKS_PER_STEP
    ri = lax.broadcasted_iota(jnp.int32, (c, c), 0)
    ci = lax.broadcasted_iota(jnp.int32, (c, c), 1)
    tri = ((ri >= ci) & (ri // RWKV_CHUNK == ci // RWKV_CHUNK)).astype(BF16)
    rspec = pl.BlockSpec((c, D_A), lambda i: (i, 0))
    return pl.pallas_call(
        _rwkv_scan_kernel,
        out_shape=[jax.ShapeDtypeStruct((t, D_A), F32), jax.ShapeDtypeStruct((H_A, HD_A, HD_A), F32)],
        grid=(t // c,),
        in_specs=[rspec] * 6 + [pl.BlockSpec((c, c), lambda i: (0, 0))],
        out_specs=[rspec, pl.BlockSpec((H_A, HD_A, HD_A), lambda i: (0, 0, 0))],
        compiler_params=_params(("arbitrary",)),
        name="rwkv_scan_prompt",
    )(r, k2, v, kkn, kka, ld, tri)


V_CHUNK = 8


def _rwkv_scan_sample_kernel(r_ref, k_ref, v_ref, kkn_ref, kka_ref, ld_ref, s_in_ref, y_ref, s_out_ref,
                             yt_ref, xt_ref, *, steps, batch):
    for q, ref in enumerate((r_ref, k_ref, v_ref, kkn_ref, kka_ref, ld_ref)):
        for t in range(steps):
            x = ref[t * batch:(t + 1) * batch, :]
            if q == 5:
                x = jnp.exp(x)
            xt_ref[q, t] = x.T

    def chunk(i, carry):
        hh = i // (HD_A // V_CHUNK)
        v0 = pl.multiple_of((i % (HD_A // V_CHUNK)) * V_CHUNK, V_CHUNK)
        base = pl.multiple_of(hh * HD_A, HD_A)
        s = s_in_ref[hh, pl.ds(v0, V_CHUNK)]
        for t in range(steps):
            rt = xt_ref[0, t, pl.ds(base, HD_A), :]
            kt = xt_ref[1, t, pl.ds(base, HD_A), :]
            vt = xt_ref[2, t, pl.ds(base + v0, V_CHUNK), :]
            kkt = xt_ref[3, t, pl.ds(base, HD_A), :]
            kat = xt_ref[4, t, pl.ds(base, HD_A), :]
            dt = xt_ref[5, t, pl.ds(base, HD_A), :]
            sa = jnp.sum(s * kkt[None], axis=1)
            s = s * dt[None] - sa[:, None, :] * kat[None] + vt[:, None, :] * kt[None]
            yt_ref[t, pl.ds(base + v0, V_CHUNK), :] = jnp.sum(s * rt[None], axis=1)
        s_out_ref[hh, pl.ds(v0, V_CHUNK)] = s
        return carry

    lax.fori_loop(0, 2 * HD_A // V_CHUNK, chunk, 0)

    for t in range(steps):
        y_ref[t * batch:(t + 1) * batch, :] = yt_ref[t].T


def _rwkv_scan_sample(r, k2, v, kkn, kka, ld, state_t, layer, steps, batch):
    rows = r.shape[0]
    xspec = pl.BlockSpec((rows, LANES), lambda p: (0, p))
    return pl.pallas_call(
        functools.partial(_rwkv_scan_sample_kernel, steps=steps, batch=batch),
        out_shape=[jax.ShapeDtypeStruct((rows, D_A), F32),
                   jax.ShapeDtypeStruct((H_A, HD_A, HD_A, batch), F32)],
        grid=(H_A // 2,),
        in_specs=[xspec] * 6 + [pl.BlockSpec((None, 2, HD_A, HD_A, batch), lambda p: (layer, p, 0, 0, 0))],
        out_specs=[xspec, pl.BlockSpec((2, HD_A, HD_A, batch), lambda p: (p, 0, 0, 0))],
        scratch_shapes=[pltpu.VMEM((steps, LANES, batch), F32),
                        pltpu.VMEM((6, steps, LANES, batch), F32)],
        compiler_params=_params(("parallel",)),
        name="rwkv_scan_sample",
    )(r, k2, v, kkn, kka, ld, state_t)


def _sb_prompt_kernel(bias_ref, q0_ref, q1_ref, k0_ref, k1_ref, v0_ref, v1_ref, u_ref, o_ref,
                      kt_ref, vb_ref, qm_ref, acc_ref, carry_ref, before_ref, lb_ref, lkb_ref):
    i = pl.program_id(0)
    tq = SB_BLOCK
    scale = HD_B ** -0.5
    n_pairs = H_B // 2

    k_blk = jnp.concatenate([k0_ref[...], k1_ref[...]], axis=1)
    kt_ref[i] = k_blk.T.astype(BF16)
    v_blk = jnp.concatenate([v0_ref[...], v1_ref[...]], axis=1)
    q_blk = jnp.concatenate([q0_ref[...], q1_ref[...]], axis=1) * (-scale)
    lane = lax.broadcasted_iota(jnp.int32, (tq, LANES), 1)
    for p in range(n_pairs):
        vb_ref[i, p] = v_blk[:, p * LANES:(p + 1) * LANES].astype(BF16)
        qp = q_blk[:, p * LANES:(p + 1) * LANES]
        qm_ref[p, 0:tq] = jnp.where(lane < HD_B, qp, 0.0).astype(BF16)
        qm_ref[p, tq:2 * tq] = jnp.where(lane >= HD_B, qp, 0.0).astype(BF16)

    row = lax.broadcasted_iota(jnp.int32, (2 * tq, tq), 0)
    col = lax.broadcasted_iota(jnp.int32, (2 * tq, tq), 1)
    causal = col < jnp.where(row < tq, row, row - tq)
    first = lax.broadcasted_iota(jnp.int32, (2 * tq, 1), 0) < tq
    bias = [jnp.where(first, -bias_ref[2 * p], -bias_ref[2 * p + 1]) for p in range(n_pairs)]
    pairs = range(n_pairs)

    groups = [tuple(range(g, g + SB_GROUP)) for g in range(0, n_pairs, SB_GROUP)]

    def scores(j, diag, ps):
        zn = {p: _dot(qm_ref[p], kt_ref[j, p * LANES:(p + 1) * LANES, :]) + bias[p] for p in ps}
        lk = {p: _neg_softplus_of_neg(zn[p]) for p in ps}
        for p in ps:
            lb_ref[p] = lk[p] - zn[p]
        if diag:
            lk = {p: jnp.where(causal, lk[p], 0.0) for p in ps}
        for p in ps:
            lkb_ref[p] = lk[p].astype(BF16)
            before = jnp.zeros((2 * tq, LANES), F32) if diag else carry_ref[p]
            before_ref[p] = before
            carry_ref[p] = before + jnp.sum(lk[p], axis=-1, keepdims=True)

    def finish(j, diag, ps):
        within = {p: _dot(lkb_ref[p], u_ref[...]) for p in ps}
        for p in ps:
            att = jnp.exp(lb_ref[p] + within[p] + before_ref[p][:, 0:1])
            if diag:
                att = jnp.where(causal, att, 0.0)
            pv = _dot(att.astype(BF16), vb_ref[j, p])
            acc_ref[p] = pv if diag else acc_ref[p] + pv

    for ps in groups:
        scores(i, True, ps)
        finish(i, True, ps)

    @pl.when(i > 0)
    def _():
        for ps in groups:
            scores(i - 1, False, ps)

        def trip(t, c):
            j = i - 1 - t
            for ps in groups:
                finish(j, False, ps)
                scores(j - 1, False, ps)
            return c

        lax.fori_loop(0, i - 1, trip, 0)
        for ps in groups:
            finish(0, False, ps)

    for p in pairs:
        o_ref[:, p * LANES:(p + 1) * LANES] = jnp.where(lane < HD_B, acc_ref[p, 0:tq], acc_ref[p, tq:2 * tq]).astype(
            o_ref.dtype)


def _sb_prompt(u, t, sb_bias):
    tq = SB_BLOCK
    nq = t // tq
    cb = RWKV_COLS // U_COL_BLOCK
    upper = (lax.broadcasted_iota(jnp.int32, (tq, tq), 0) > lax.broadcasted_iota(jnp.int32, (tq, tq), 1)).astype(BF16)
    uspec = lambda c: pl.BlockSpec((tq, U_COL_BLOCK), functools.partial(lambda i, c: (i, c), c=c))
    return pl.pallas_call(
        _sb_prompt_kernel,
        out_shape=jax.ShapeDtypeStruct((t, D_B), BF16),
        grid=(nq,),
        in_specs=[pl.BlockSpec(memory_space=pltpu.SMEM)] + [uspec(cb + c) for c in range(6)]
                 + [pl.BlockSpec((tq, tq), lambda i: (0, 0))],
        out_specs=pl.BlockSpec((tq, D_B), lambda i: (i, 0)),
        scratch_shapes=[pltpu.VMEM((nq, D_B, tq), BF16), pltpu.VMEM((nq, H_B // 2, tq, LANES), BF16),
                        pltpu.VMEM((H_B // 2, 2 * tq, LANES), BF16), pltpu.VMEM((H_B // 2, 2 * tq, LANES), F32),
                        pltpu.VMEM((H_B // 2, 2 * tq, LANES), F32), pltpu.VMEM((H_B // 2, 2 * tq, LANES), F32),
                        pltpu.VMEM((H_B // 2, 2 * tq, tq), F32), pltpu.VMEM((H_B // 2, 2 * tq, tq), BF16)],
        compiler_params=_params(("arbitrary",)),
        name="sb_prompt",
    )(sb_bias, u, u, u, u, u, u, upper)


Q_PAD = 8


def _sb_sample_kernel(pt_ref, bias_ref, qkv_ref, u_ref, *refs, n_pages, steps):
    k_refs = refs[:n_pages]
    v_refs = refs[n_pages:2 * n_pages]
    o_ref = refs[2 * n_pages]
    scale = HD_B ** -0.5
    rows = H_B * Q_PAD
    x = qkv_ref[0]
    q8 = x[:, 0:D_B] * scale
    zpad = jnp.zeros((PAGE_SIZE - Q_PAD, D_B), F32)
    k_new = jnp.concatenate([x[:, D_B:2 * D_B], zpad], axis=0).astype(BF16)
    v_new = jnp.concatenate([x[:, 2 * D_B:3 * D_B], zpad], axis=0).astype(BF16)
    lane_head = lax.broadcasted_iota(jnp.int32, (Q_PAD, D_B), 1) // HD_B
    qbd = jnp.concatenate([jnp.where(lane_head == h, q8, 0.0) for h in range(H_B)], axis=0).astype(BF16)
    bias = jnp.concatenate([jnp.full((Q_PAD, 1), bias_ref[h], F32) for h in range(H_B)], axis=0)

    page2d = lambda ref: ref[...].reshape(D_B, PAGE_SIZE).astype(BF16)
    z = jnp.concatenate([_dot(qbd, page2d(k_refs[p])) for p in range(n_pages)], axis=1) + bias
    zn = _dot_nt(qbd, k_new) + bias
    t_q = lax.broadcasted_iota(jnp.int32, (rows, PAGE_SIZE), 0) % Q_PAD
    c_k = lax.broadcasted_iota(jnp.int32, (rows, PAGE_SIZE), 1)
    new_ok = c_k < jnp.minimum(t_q, steps)

    lk = _neg_softplus(z)
    lkn_raw = _neg_softplus(zn)
    lkn = jnp.where(new_ok, lkn_raw, 0.0)
    upper = u_ref[...]
    later_n = _dot(lkn.astype(BF16), upper[:PAGE_SIZE, :PAGE_SIZE])
    att_n = jnp.where(new_ok, jnp.exp(zn + lkn_raw + later_n), 0.0)
    carry = jnp.sum(lkn, axis=-1, keepdims=True)
    blk = upper.shape[0]
    n_blk = n_pages * PAGE_SIZE // blk
    laters = [None] * n_blk
    for b in reversed(range(n_blk)):
        lkb = lk[:, b * blk:(b + 1) * blk]
        laters[b] = _dot(lkb.astype(BF16), upper) + carry
        carry = carry + jnp.sum(lkb, axis=-1, keepdims=True)
    att = jnp.exp(z + lk + jnp.concatenate(laters, axis=1)).astype(BF16)

    out = _dot(att_n.astype(BF16), v_new)
    for p in range(n_pages):
        out = out + _dot_nt(att[:, p * PAGE_SIZE:(p + 1) * PAGE_SIZE], page2d(v_refs[p]))
    y = jnp.zeros((Q_PAD, D_B), F32)
    for h in range(H_B):
        y = y + jnp.where(lane_head == h, out[h * Q_PAD:(h + 1) * Q_PAD], 0.0)
    o_ref[0] = y


def _sb_sample(qkv_pad, cache_kt, cache_vt, page_table, sb_bias, layer, steps):
    batch, n_pages = page_table.shape
    blk = MXU_DIM
    upper = (lax.broadcasted_iota(jnp.int32, (blk, blk), 0) > lax.broadcasted_iota(jnp.int32, (blk, blk), 1)).astype(BF16)
    page = lambda p: pl.BlockSpec((None, None, H_B, HD_B, PAGE_SIZE),
                                  functools.partial(lambda b, pt, p: (layer, pt[b, p], 0, 0, 0), p=p))
    grid_spec = pltpu.PrefetchScalarGridSpec(
        num_scalar_prefetch=1,
        grid=(batch,),
        in_specs=[pl.BlockSpec(memory_space=pltpu.SMEM),
                  pl.BlockSpec((1, Q_PAD, 3 * D_B), lambda b, pt: (b, 0, 0)),
                  pl.BlockSpec((blk, blk), lambda b, pt: (0, 0))]
                 + [page(p) for p in range(n_pages)] * 2,
        out_specs=pl.BlockSpec((1, Q_PAD, D_B), lambda b, pt: (b, 0, 0)),
    )
    return pl.pallas_call(
        functools.partial(_sb_sample_kernel, n_pages=n_pages, steps=steps),
        out_shape=jax.ShapeDtypeStruct((batch, Q_PAD, D_B), F32),
        grid_spec=grid_spec,
        compiler_params=_params(("parallel",)),
        name="sb_sample",
    )(page_table, sb_bias, qkv_pad, upper, *([cache_kt] * n_pages), *([cache_vt] * n_pages))


def _pool_mix(sums, x, cnts, wp_ref, scale_ref):
    outs = []
    for g in range(len(POOL_WINDOWS)):
        pooled = sums[g] / cnts[g] - x[:, g * C_POOL:(g + 1) * C_POOL]
        outs.append(_dot(pooled.astype(BF16), wp_ref[g].astype(BF16)))
    return jnp.concatenate(outs, axis=1) * scale_ref[...]


def _pool_prompt_kernel(u0_ref, u1_ref, wp_ref, scale_ref, o_ref, ext_ref):
    tp = u0_ref.shape[0]
    pad = 16
    i = pl.program_id(0)

    @pl.when(i == 0)
    def _():
        ext_ref[0:pad, :] = jnp.zeros((pad, D_C), F32)

    x = jnp.concatenate([u0_ref[...], u1_ref[...]], axis=1)
    ext_ref[pad:pad + tp, :] = x
    pos = i * tp + lax.broadcasted_iota(jnp.int32, (tp, 1), 0)
    sums, cnts = [], []
    for g, w in enumerate(POOL_WINDOWS):
        s = x[:, g * C_POOL:(g + 1) * C_POOL]
        for back in range(1, w):
            s = s + ext_ref[pad - back:pad - back + tp, g * C_POOL:(g + 1) * C_POOL]
        sums.append(s)
        cnts.append(jnp.minimum(w, pos + 1).astype(F32))
    o_ref[...] = _pool_mix(sums, x, cnts, wp_ref, scale_ref).astype(o_ref.dtype)
    ext_ref[0:pad, :] = x[tp - pad:, :]


def _pool_prompt(u, t, w_pool, pool_scale, tp):
    cb = (RWKV_COLS + 3 * D_B) // U_COL_BLOCK
    uspec = lambda c: pl.BlockSpec((tp, U_COL_BLOCK), functools.partial(lambda i, c: (i, c), c=c))
    return pl.pallas_call(
        _pool_prompt_kernel,
        out_shape=jax.ShapeDtypeStruct((t, D_C), BF16),
        grid=(t // tp,),
        in_specs=[uspec(cb), uspec(cb + 1),
                  pl.BlockSpec((len(POOL_WINDOWS), C_POOL, C_POOL), lambda i: (0, 0, 0)),
                  pl.BlockSpec((1, D_C), lambda i: (0, 0))],
        out_specs=pl.BlockSpec((tp, D_C), lambda i: (i, 0)),
        scratch_shapes=[pltpu.VMEM((16 + tp, D_C), F32)],
        compiler_params=_params(("arbitrary",)),
        name="pool_prompt",
    )(u, u, w_pool, pool_scale.reshape(1, D_C))


def _pool_sample_kernel(u0_ref, u1_ref, pre_ref, wp_ref, scale_ref, o_ref, *, steps, batch, start_pos):
    x_all = jnp.concatenate([u0_ref[...], u1_ref[...]], axis=1)
    ext = [pre_ref[r] for r in range(POOL_BUF)] + [x_all[t * batch:(t + 1) * batch] for t in range(steps)]
    for t in range(steps):
        sums, cnts = [], []
        for g, w in enumerate(POOL_WINDOWS):
            s = ext[POOL_BUF + t][:, g * C_POOL:(g + 1) * C_POOL]
            for back in range(1, w):
                s = s + ext[POOL_BUF + t - back][:, g * C_POOL:(g + 1) * C_POOL]
            sums.append(s)
            cnts.append(float(min(w, start_pos + t + 1)))
        o_ref[t * batch:(t + 1) * batch, :] = _pool_mix(sums, ext[POOL_BUF + t], cnts, wp_ref, scale_ref).astype(
            o_ref.dtype)


def _pool_sample(u, row_block0, prefix_tm, w_pool, pool_scale, steps, batch, start_pos):
    rows = steps * batch
    cb = (RWKV_COLS + 3 * D_B) // U_COL_BLOCK
    uspec = lambda c: pl.BlockSpec((rows, U_COL_BLOCK), functools.partial(lambda i, c: (row_block0, c), c=c))
    return pl.pallas_call(
        functools.partial(_pool_sample_kernel, steps=steps, batch=batch, start_pos=start_pos),
        out_shape=jax.ShapeDtypeStruct((rows, D_C), BF16),
        grid=(1,),
        in_specs=[uspec(cb), uspec(cb + 1),
                  pl.BlockSpec((POOL_BUF, batch, D_C), lambda i: (0, 0, 0)),
                  pl.BlockSpec((len(POOL_WINDOWS), C_POOL, C_POOL), lambda i: (0, 0, 0)),
                  pl.BlockSpec((1, D_C), lambda i: (0, 0))],
        out_specs=pl.BlockSpec((rows, D_C), lambda i: (0, 0)),
        compiler_params=_params(("arbitrary",)),
        name="pool_sample",
    )(u, u, prefix_tm, w_pool, pool_scale.reshape(1, D_C))


DENSE_SUB_ROWS = 272
TM_DENSE = 1088
TM_WIDE = 2176
TM_DOWN = 544


def kernel(x_prompt, x_sample, p_prompt, p_sample, cache_k, cache_v, page_table, state_wkv, state_shift, state_pool, g_mix, w_in, mu_shift, w0, w_lora_up, a0, a_lora_up, g_lora_up, k_k, k_a, r_k, ln_w, ln_b, sb_bias, w_pool, pool_scale, w_o, g_ffn, w_gate, w_up, w_down, g_ple, w_pg, w_pp, g_final):
    depth = w_in.shape[0]
    tp = x_prompt.shape[1]
    bs, ts = x_sample.shape[0], x_sample.shape[1]
    rows_s = bs * ts
    m = tp + rows_s
    n_pages = page_table.shape[1]
    past_len = n_pages * PAGE_SIZE

    to_tm = lambda x: jnp.swapaxes(x, 0, 1).reshape(rows_s, x.shape[-1])
    from_tm = lambda x: jnp.swapaxes(x.reshape(ts, bs, x.shape[-1]), 0, 1)

    h = jnp.concatenate([x_prompt[0], to_tm(x_sample)], axis=0)
    cache_kt = jnp.transpose(cache_k, (0, 1, 3, 4, 2))
    cache_vt = jnp.transpose(cache_v, (0, 1, 3, 4, 2))
    state_t = jnp.transpose(state_wkv, (0, 2, 3, 4, 1))

    outs = {n: [] for n in ("kp", "vp", "ks", "vs", "wp", "ws", "sp", "ss", "pp", "ps")}
    for i in range(depth):
        lw = dict(mu_shift=mu_shift[i], w0=w0[i], w_lora_up=w_lora_up[i], a0=a0[i], a_lora_up=a_lora_up[i],
                  g_lora_up=g_lora_up[i], k_k=k_k[i], k_a=k_a[i], r_k=r_k[i].reshape(-1), ln_w=ln_w[i], ln_b=ln_b[i])
        xn = _rmsnorm(h, g_mix[i], BF16, TM_DENSE)
        u = _dense([(xn, w_in, D_MODEL, 0, 0)], i, 1, _epi_plain, None, TM_WIDE, 768, F32, "dense_in")

        prep_p = _rwkv_prep(u, 0, tp, jnp.zeros((1, RWKV_COLS), F32), 1, 256, lw)
        y_p, wkv_p = _rwkv_scan_prompt(*prep_p[:6])
        ya_p = _rwkv_post(y_p, prep_p[7], prep_p[6], lw, 512)
        prep_s = _rwkv_prep(u, tp // rows_s, rows_s, state_shift[i], bs, rows_s, lw)
        y_s, wkv_s = _rwkv_scan_sample(*prep_s[:6], state_t, i, ts, bs)
        ya_s = _rwkv_post(y_s, prep_s[7], prep_s[6], lw, rows_s)

        yb_p = _sb_prompt(u, tp, sb_bias[i])
        qkv_s = from_tm(u[tp:, RWKV_COLS:RWKV_COLS + 3 * D_B])
        qkv_pad = jnp.pad(qkv_s, ((0, 0), (0, Q_PAD - ts), (0, 0)))
        yb_s = _sb_sample(qkv_pad, cache_kt, cache_vt, page_table, sb_bias[i], i, ts)
        yb_s = to_tm(yb_s[:, :ts]).astype(BF16)

        yc_p = _pool_prompt(u, tp, w_pool[i], pool_scale[i], 512)
        yc_s = _pool_sample(u, tp // rows_s, jnp.swapaxes(state_pool[i], 0, 1), w_pool[i], pool_scale[i],
                            ts, bs, past_len)

        ya = jnp.concatenate([ya_p, ya_s], axis=0)
        yb = jnp.concatenate([yb_p, yb_s], axis=0)
        yc = jnp.concatenate([yc_p, yc_s], axis=0)
        h = _dense([(ya, w_o, D_A, 0, 0), (yb, w_o, D_B, D_A // D_B, 0),
                    (yc, w_o, D_C, (D_A + D_B) // D_C, 0)], i, 1, _epi_residual, h, TM_WIDE, 512, F32, "dense_o")

        xn = _rmsnorm(h, g_ffn[i], BF16, TM_DENSE)
        act = _dense([(xn, w_gate, D_MODEL, 0, 0), (xn, w_up, D_MODEL, 0, 1)], i, 2, _epi_swiglu, None,
                     TM_WIDE, 512, BF16, "dense_ffn_in")
        h = _dense([(act, w_down, D_FF, 0, 0)], i, 1, _epi_residual, h, TM_DOWN, 512, F32, "dense_ffn_out")

        xn = _rmsnorm(h, g_ple[i], BF16, TM_DENSE)
        p_i = jnp.concatenate([p_prompt[i, 0], to_tm(p_sample[i])], axis=0).astype(BF16)
        h = _dense([(xn, w_pg, D_MODEL, 0, 0), (p_i, w_pp, D_PLE, 0, 1)], i, 2, _epi_gated_residual, h,
                   TM_WIDE, 512, F32, "dense_ple")

        kv_p = u[:tp, RWKV_COLS + D_B:RWKV_COLS + 3 * D_B]
        outs["kp"].append(kv_p[:, :D_B].reshape(1, tp, H_B, HD_B))
        outs["vp"].append(kv_p[:, D_B:].reshape(1, tp, H_B, HD_B))
        outs["ks"].append(qkv_s[..., D_B:2 * D_B].reshape(bs, ts, H_B, HD_B))
        outs["vs"].append(qkv_s[..., 2 * D_B:].reshape(bs, ts, H_B, HD_B))
        outs["wp"].append(wkv_p.reshape(1, H_A, HD_A, HD_A))
        outs["ws"].append(jnp.transpose(wkv_s, (3, 0, 1, 2)))
        outs["sp"].append(u[tp - 1:tp, :RWKV_COLS])
        outs["ss"].append(u[m - bs:, :RWKV_COLS])
        uc_s = from_tm(u[tp:, RWKV_COLS + 3 * D_B:])
        outs["pp"].append(u[tp - POOL_BUF:tp, RWKV_COLS + 3 * D_B:][None])
        outs["ps"].append(jnp.concatenate([state_pool[i][:, ts:], uc_s], axis=1))

    y_prompt = _rmsnorm(h, g_final, F32, rows_s, 0, tp)[None]
    y_sample = from_tm(_rmsnorm(h, g_final, F32, rows_s, tp // rows_s, rows_s))
    stack = lambda n: jnp.stack(outs[n])
    return (y_prompt, y_sample, stack("kp"), stack("vp"), stack("ks"), stack("vs"), stack("wp"), stack("ws"),
            stack("sp"), stack("ss"), stack("pp"), stack("ps"))
```

```python
import functools

import jax
import jax.numpy as jnp
from jax import lax
from jax.experimental import pallas as pl
from jax.experimental.pallas import tpu as pltpu

F32 = jnp.float32
BF16 = jnp.bfloat16

D_MODEL = 2048
D_A = 1024
HD_A = 64
H_A = 16
LORA_W = 64
LORA_A = 64
LORA_G = 128
D_B = 512
HD_B = 64
H_B = 8
D_C = 512
POOL_WINDOWS = (2, 4, 8, 16)
C_POOL = 128
POOL_BUF = 15
RWKV_COLS = 3 * D_A + LORA_W + LORA_A + LORA_G
IN_COLS = RWKV_COLS + 3 * D_B + D_C
D_FF = 5632
D_PLE = 256
PAGE_SIZE = 128
EPS = 1e-6
GN_EPS = 64e-5

LANES = 128
MXU_DIM = 256
RWKV_CHUNK = 64
RWKV_CHUNKS_PER_STEP = 4
SB_BLOCK = 256
SB_GROUP = 1
U_COL_BLOCK = 256
VMEM_LIMIT_MB = 56
NEG_LOG2E = -1.4426950408889634


def _params(sem, vmem_mb=VMEM_LIMIT_MB):
    return pltpu.CompilerParams(dimension_semantics=sem, vmem_limit_bytes=vmem_mb * 2**20)


def _dot(a, b):
    return jnp.dot(a, b, preferred_element_type=F32)


def _dot_nt(a, b):
    return lax.dot_general(a, b, (((1,), (1,)), ((), ())), preferred_element_type=F32)


def _split(x):
    hi = x.astype(BF16)
    lo = (x - hi.astype(F32)).astype(BF16)
    return hi, lo


def _dot_left_exact(m_bf16, x):
    hi, lo = _split(x)
    return _dot(m_bf16, hi) + _dot(m_bf16, lo)


def _dot_right_exact(x, m_bf16):
    hi, lo = _split(x)
    return _dot(hi, m_bf16) + _dot(lo, m_bf16)


def _neg_softplus(z):
    return _neg_softplus_of_neg(-z)


def _neg_softplus_of_neg(zn):
    e = jnp.exp2(jnp.abs(zn) * NEG_LOG2E)
    return jnp.minimum(zn, 0.0) - jnp.log(1.0 + e)


def _sigmoid(x):
    return 1.0 / (1.0 + jnp.exp(-x))


def _rmsnorm_kernel(h_ref, g_ref, o_ref):
    x = h_ref[...]
    y = x * lax.rsqrt(jnp.mean(x * x, axis=-1, keepdims=True) + EPS)
    o_ref[...] = (y * g_ref[...]).astype(o_ref.dtype)


def _rmsnorm(h, g, out_dtype, tm, row_block0=0, rows=None):
    d = h.shape[1]
    m = h.shape[0] if rows is None else rows
    return pl.pallas_call(
        _rmsnorm_kernel,
        out_shape=jax.ShapeDtypeStruct((m, d), out_dtype),
        grid=(m // tm,),
        in_specs=[pl.BlockSpec((tm, d), lambda i: (i + row_block0, 0)), pl.BlockSpec((1, d), lambda i: (0, 0))],
        out_specs=pl.BlockSpec((tm, d), lambda i: (i, 0)),
        compiler_params=_params(("parallel",)),
        name="rmsnorm",
    )(h, g.reshape(1, d))


def _dense_kernel(*refs, n_terms, a_of, acc_of, n_acc, has_res, epilogue):
    n_a = max(a_of) + 1
    a_refs = [refs[a_of[t]] for t in range(n_terms)]
    w_refs = refs[n_a:n_a + n_terms]
    pos = n_a + n_terms
    res_ref = refs[pos] if has_res else None
    pos += int(has_res)
    o_ref = refs[pos]
    wbf_refs = refs[pos + 1:]

    @pl.when(pl.program_id(1) == 0)
    def _():
        for t in range(n_terms):
            wbf_refs[t][...] = w_refs[t][...].astype(BF16)

    tm = o_ref.shape[0]
    for r0 in range(0, tm, DENSE_SUB_ROWS):
        rows = slice(r0, r0 + DENSE_SUB_ROWS)
        accs = [None] * n_acc
        for t in range(n_terms):
            d = _dot(a_refs[t][rows, :], wbf_refs[t][...])
            accs[acc_of[t]] = d if accs[acc_of[t]] is None else accs[acc_of[t]] + d
        res = res_ref[rows, :] if has_res else None
        o_ref[rows, :] = epilogue(accs, res).astype(o_ref.dtype)


def _dense(terms, layer, n_acc, epilogue, res, tm, tn, out_dtype, name):
    m = terms[0][0].shape[0]
    n = terms[0][1].shape[2]
    n_terms = len(terms)
    acts = []
    a_of = []
    for t in terms:
        if not any(t[0] is a for a in acts):
            acts.append(t[0])
        a_of.append([t[0] is a for a in acts].index(True))
    in_specs = [pl.BlockSpec((tm, a.shape[1]), lambda j, i: (i, 0)) for a in acts]
    in_specs += [pl.BlockSpec((None, t[2], tn), functools.partial(lambda j, i, r: (layer, r, j), r=t[3]))
                 for t in terms]
    args = acts + [t[1] for t in terms]
    if res is not None:
        in_specs.append(pl.BlockSpec((tm, tn), lambda j, i: (i, j)))
        args.append(res)
    kern = functools.partial(_dense_kernel, n_terms=n_terms, a_of=a_of, acc_of=[t[4] for t in terms], n_acc=n_acc,
                             has_res=res is not None, epilogue=epilogue)
    return pl.pallas_call(
        kern,
        out_shape=jax.ShapeDtypeStruct((m, n), out_dtype),
        grid=(n // tn, m // tm),
        in_specs=in_specs,
        out_specs=pl.BlockSpec((tm, tn), lambda j, i: (i, j)),
        scratch_shapes=[pltpu.VMEM((t[2], tn), BF16) for t in terms],
        compiler_params=_params(("arbitrary", "arbitrary")),
        name=name,
    )(*args)


def _epi_plain(accs, res):
    return accs[0]


def _epi_residual(accs, res):
    return res + accs[0]


def _epi_swiglu(accs, res):
    gate, up = accs
    return gate * _sigmoid(gate) * up


def _epi_gated_residual(accs, res):
    return res + _sigmoid(accs[0]) * accs[1]


def _seg_mats():
    row_head = lax.broadcasted_iota(jnp.int32, (D_A, LANES), 0) // HD_A
    col = lax.broadcasted_iota(jnp.int32, (D_A, LANES), 1)
    e1 = (row_head == col).astype(BF16)
    return e1, e1.T


def _segsum(x, e1, e2):
    return _dot_right_exact(_dot_right_exact(x, e1), e2)


def _rwkv_prep_kernel(u_ref, prev_ref, mu_ref, w0_ref, a0_ref, kk_ref, ka_ref, rk_ref,
                      wl_ref, al_ref, gl_ref, e1_ref, e2_ref,
                      r_o, k_o, v_o, kkn_o, kka_o, ld_o, g_o, bonus_o, carry_ref, *, lag):
    rows = u_ref.shape[0]

    @pl.when(pl.program_id(0) == 0)
    def _():
        carry_ref[...] = prev_ref[...]

    u = u_ref[...]
    if lag == 1:
        rolled = pltpu.roll(u, 1, axis=0)
        row = lax.broadcasted_iota(jnp.int32, u.shape, 0)
        shifted = jnp.where(row == 0, carry_ref[...], rolled)
        carry_ref[...] = u[rows - 1:rows, :]
    else:
        shifted = jnp.concatenate([carry_ref[...], u[:rows - lag, :]], axis=0)
        carry_ref[...] = u[rows - lag:, :]
    xs = u + (shifted - u) * mu_ref[...]

    r = xs[:, 0:D_A]
    k = xs[:, D_A:2 * D_A]
    v = xs[:, 2 * D_A:3 * D_A]
    x_wa = xs[:, 3 * D_A:3 * D_A + LORA_W + LORA_A]
    xg = xs[:, 3 * D_A + LORA_W + LORA_A:]
    lane = lax.broadcasted_iota(jnp.int32, x_wa.shape, 1)
    tw = jnp.where(lane < LORA_W, jnp.tanh(x_wa), 0.0).astype(BF16)
    xa = jnp.where(lane >= LORA_W, x_wa, 0.0).astype(BF16)
    w_lin = w0_ref[...] + _dot(tw, wl_ref[...].astype(BF16))
    w = _neg_softplus(-w_lin) - 0.5
    ld_o[...] = -jnp.exp(w)
    a = _sigmoid(a0_ref[...] + _dot(xa, al_ref[...].astype(BF16)))
    g_o[...] = _dot(_sigmoid(xg).astype(BF16), gl_ref[...].astype(BF16))

    e1 = e1_ref[...]
    e2 = e2_ref[...]
    kk = k * kk_ref[...]
    norm = jnp.maximum(jnp.sqrt(_segsum(kk * kk, e1, e2)), 1e-12)
    kkn = kk / norm
    k2 = k * (1.0 + (a - 1.0) * ka_ref[...])
    r_o[...] = r
    k_o[...] = k2
    v_o[...] = v
    kkn_o[...] = kkn
    kka_o[...] = kkn * a
    bonus_o[...] = _segsum(r * k2 * rk_ref[...], e1, e2) * v


def _rwkv_prep(u, row_block0, rows, prev, lag, tr, lw):
    e1, e2 = _seg_mats()
    zeros_w = jnp.zeros((LORA_W, D_A), F32)
    wl = jnp.concatenate([lw["w_lora_up"], zeros_w], axis=0)
    al = jnp.concatenate([zeros_w, lw["a_lora_up"]], axis=0)
    vec = lambda x: x.reshape(1, -1)
    const = lambda shape: pl.BlockSpec(shape, lambda i: (0, 0))
    out = jax.ShapeDtypeStruct((rows, D_A), F32)
    ospec = pl.BlockSpec((tr, D_A), lambda i: (i, 0))
    return pl.pallas_call(
        functools.partial(_rwkv_prep_kernel, lag=lag),
        out_shape=[out] * 8,
        grid=(rows // tr,),
        in_specs=[pl.BlockSpec((tr, RWKV_COLS), lambda i: (i + row_block0, 0)),
                  const((lag, RWKV_COLS)), const((1, RWKV_COLS)),
                  const((1, D_A)), const((1, D_A)), const((1, D_A)), const((1, D_A)), const((1, D_A)),
                  const((LORA_W + LORA_A, D_A)), const((LORA_W + LORA_A, D_A)), const((LORA_G, D_A)),
                  const((D_A, LANES)), const((LANES, D_A))],
        out_specs=[ospec] * 8,
        scratch_shapes=[pltpu.VMEM((lag, RWKV_COLS), F32)],
        compiler_params=_params(("arbitrary",)),
        name="rwkv_prep",
    )(u, prev, vec(lw["mu_shift"]), vec(lw["w0"]), vec(lw["a0"]), vec(lw["k_k"]), vec(lw["k_a"]),
      vec(lw["r_k"]), wl, al, lw["g_lora_up"], e1, e2)


def _rwkv_post_kernel(y_ref, bonus_ref, g_ref, lnw_ref, lnb_ref, e1_ref, e2_ref, o_ref):
    e1 = e1_ref[...]
    e2 = e2_ref[...]
    y = y_ref[...]
    mean = _segsum(y, e1, e2) * (1.0 / HD_A)
    d = y - mean
    var = _segsum(d * d, e1, e2) * (1.0 / HD_A)
    yn = d * lax.rsqrt(var + GN_EPS) * lnw_ref[...] + lnb_ref[...]
    o_ref[...] = ((yn + bonus_ref[...]) * g_ref[...]).astype(o_ref.dtype)


def _rwkv_post(y, bonus, g, lw, tr):
    rows = y.shape[0]
    e1, e2 = _seg_mats()
    rspec = pl.BlockSpec((tr, D_A), lambda i: (i, 0))
    const = lambda shape: pl.BlockSpec(shape, lambda i: (0, 0))
    return pl.pallas_call(
        _rwkv_post_kernel,
        out_shape=jax.ShapeDtypeStruct((rows, D_A), BF16),
        grid=(rows // tr,),
        in_specs=[rspec, rspec, rspec, const((1, D_A)), const((1, D_A)), const((D_A, LANES)), const((LANES, D_A))],
        out_specs=rspec,
        compiler_params=_params(("parallel",)),
        name="rwkv_post",
    )(y, bonus, g, lw["ln_w"].reshape(1, D_A), lw["ln_b"].reshape(1, D_A), e1, e2)


def _unit_lower_inverses(ls, n):
    row = lax.broadcasted_iota(jnp.int32, (n, n), 0)
    col = lax.broadcasted_iota(jnp.int32, (n, n), 1)
    eye = (row == col).astype(F32)
    ts = [eye + l for l in ls]
    ps = [l.astype(BF16) for l in ls]
    n_factors = n.bit_length() - 1
    for stage in range(1, n_factors):
        ps = [_dot(p, p).astype(BF16) for p in ps]
        ts = [t + _dot(t.astype(BF16), p) for t, p in zip(ts, ps)]
    return ts


def _rwkv_scan_kernel(r_ref, k_ref, v_ref, kkn_ref, kka_ref, ld_ref, bonus_ref, g_ref, lnw_ref, lnb_ref, tri_ref,
                      y_ref, s_ref):
    c = RWKV_CHUNK
    n_sub = RWKV_CHUNKS_PER_STEP
    heads = range(H_A)
    sl = [slice(h * HD_A, (h + 1) * HD_A) for h in heads]

    @pl.when(pl.program_id(0) == 0)
    def _():
        s_ref[...] = jnp.zeros_like(s_ref)

    ld_all = ld_ref[...]
    cum_all = _dot_left_exact(tri_ref[...], ld_all)
    row = lax.broadcasted_iota(jnp.int32, (2 * c, 2 * c), 0)
    col = lax.broadcasted_iota(jnp.int32, (2 * c, 2 * c), 1)
    keep = col % c < row % c + jnp.where(row < c, 0, 1)
    zeros_cv = jnp.zeros((c, HD_A), F32)

    pre = []
    for j in range(n_sub):
        rows = slice(j * c, (j + 1) * c)
        ld, cum = ld_all[rows], cum_all[rows]
        last = cum[c - 1:c, :]
        p_inv = jnp.exp(-cum)
        p_end = jnp.exp(last - cum)
        kkn, kka, k2, v = kkn_ref[rows, :], kka_ref[rows, :], k_ref[rows, :], v_ref[rows, :]
        a_t = -kkn * jnp.exp(cum - ld)
        r_t = r_ref[rows, :] * jnp.exp(cum)
        b_t, k_t, b_e, k_e = kka * p_inv, k2 * p_inv, kka * p_end, k2 * p_end
        ar = [jnp.concatenate([a_t[:, sl[h]], r_t[:, sl[h]]], axis=0).astype(BF16) for h in heads]
        bk = [jnp.concatenate([b_t[:, sl[h]], k_t[:, sl[h]]], axis=0).astype(BF16) for h in heads]
        be = [jnp.concatenate([b_e[:, sl[h]], k_e[:, sl[h]]], axis=0).astype(BF16) for h in heads]
        pre.append(dict(ar=ar, bk=bk, be=be, vh=[v[:, sl[h]] for h in heads], p_tot=jnp.exp(last)))
    for p in pre:
        p["gm"] = [jnp.where(keep, _dot_nt(p["ar"][h], p["bk"][h]), 0.0) for h in heads]
        p["gmb"] = [g.astype(BF16) for g in p["gm"]]
    for p in pre:
        p["lakv"] = [_dot(p["gmb"][h][:c, :], jnp.concatenate([zeros_cv, p["vh"][h]], axis=0).astype(BF16))
                     for h in heads]
    tinv = _unit_lower_inverses([p["gm"][h][:c, :c] for p in pre for h in heads], c)
    tinv = [t.astype(BF16) for t in tinv]

    s = [s_ref[h] for h in heads]
    for j, p in enumerate(pre):
        ars = [_dot_nt(p["ar"][h], s[h].astype(BF16)) for h in heads]
        x = [ars[h][:c] + p["lakv"][h] for h in heads]
        w = [_dot(tinv[j * H_A + h], x[h].astype(BF16)) for h in heads]
        wv = [jnp.concatenate([w[h], p["vh"][h]], axis=0) for h in heads]
        ys = [ars[h][c:] + _dot(p["gmb"][h][c:, :], wv[h].astype(BF16)) for h in heads]
        s = [s[h] * p["p_tot"][:, sl[h]] + _dot(wv[h].T.astype(BF16), p["be"][h]) for h in heads]
        mean = [jnp.mean(y, axis=-1, keepdims=True) for y in ys]
        dev = [ys[h] - mean[h] for h in heads]
        var = [jnp.mean(d * d, axis=-1, keepdims=True) for d in dev]
        yn = jnp.concatenate([dev[h] * lax.rsqrt(var[h] + GN_EPS) for h in heads], axis=1)
        rows = slice(j * c, (j + 1) * c)
        out = (yn * lnw_ref[...] + lnb_ref[...] + bonus_ref[rows, :]) * g_ref[rows, :]
        y_ref[rows, :] = out.astype(y_ref.dtype)
    for h in heads:
        s_ref[h] = s[h]


def _rwkv_scan_prompt(r, k2, v, kkn, kka, ld, bonus, g, ln_w, ln_b):
    t = r.shape[0]
    c = RWKV_CHUNK * RWKV_CHUNKS_PER_STEP
    ri = lax.broadcasted_iota(jnp.int32, (c, c), 0)
    ci = lax.broadcasted_iota(jnp.int32, (c, c), 1)
    tri = ((ri >= ci) & (ri // RWKV_CHUNK == ci // RWKV_CHUNK)).astype(BF16)
    rspec = pl.BlockSpec((c, D_A), lambda i: (i, 0))
    return pl.pallas_call(
        _rwkv_scan_kernel,
        out_shape=[jax.ShapeDtypeStruct((t, D_A), BF16), jax.ShapeDtypeStruct((H_A, HD_A, HD_A), F32)],
        grid=(t // c,),
        in_specs=[rspec] * 8 + [pl.BlockSpec((1, D_A), lambda i: (0, 0))] * 2 + [pl.BlockSpec((c, c), lambda i: (0, 0))],
        out_specs=[rspec, pl.BlockSpec((H_A, HD_A, HD_A), lambda i: (0, 0, 0))],
        compiler_params=_params(("arbitrary",)),
        name="rwkv_scan_prompt",
    )(r, k2, v, kkn, kka, ld, bonus, g, ln_w.reshape(1, D_A), ln_b.reshape(1, D_A), tri)


V_CHUNK = 8


def _rwkv_scan_sample_kernel(r_ref, k_ref, v_ref, kkn_ref, kka_ref, ld_ref, s_in_ref, y_ref, s_out_ref,
                             yt_ref, xt_ref, *, steps, batch):
    for q, ref in enumerate((r_ref, k_ref, v_ref, kkn_ref, kka_ref, ld_ref)):
        for t in range(steps):
            x = ref[t * batch:(t + 1) * batch, :]
            if q == 5:
                x = jnp.exp(x)
            xt_ref[q, t] = x.T

    def chunk(i, carry):
        hh = i // (HD_A // V_CHUNK)
        v0 = pl.multiple_of((i % (HD_A // V_CHUNK)) * V_CHUNK, V_CHUNK)
        base = pl.multiple_of(hh * HD_A, HD_A)
        s = s_in_ref[hh, pl.ds(v0, V_CHUNK)]
        for t in range(steps):
            rt = xt_ref[0, t, pl.ds(base, HD_A), :]
            kt = xt_ref[1, t, pl.ds(base, HD_A), :]
            vt = xt_ref[2, t, pl.ds(base + v0, V_CHUNK), :]
            kkt = xt_ref[3, t, pl.ds(base, HD_A), :]
            kat = xt_ref[4, t, pl.ds(base, HD_A), :]
            dt = xt_ref[5, t, pl.ds(base, HD_A), :]
            sa = jnp.sum(s * kkt[None], axis=1)
            s = s * dt[None] - sa[:, None, :] * kat[None] + vt[:, None, :] * kt[None]
            yt_ref[t, pl.ds(base + v0, V_CHUNK), :] = jnp.sum(s * rt[None], axis=1)
        s_out_ref[hh, pl.ds(v0, V_CHUNK)] = s
        return carry

    lax.fori_loop(0, 2 * HD_A // V_CHUNK, chunk, 0)

    for t in range(steps):
        y_ref[t * batch:(t + 1) * batch, :] = yt_ref[t].T


def _rwkv_scan_sample(r, k2, v, kkn, kka, ld, state_t, layer, steps, batch):
    rows = r.shape[0]
    xspec = pl.BlockSpec((rows, LANES), lambda p: (0, p))
    return pl.pallas_call(
        functools.partial(_rwkv_scan_sample_kernel, steps=steps, batch=batch),
        out_shape=[jax.ShapeDtypeStruct((rows, D_A), F32),
                   jax.ShapeDtypeStruct((H_A, HD_A, HD_A, batch), F32)],
        grid=(H_A // 2,),
        in_specs=[xspec] * 6 + [pl.BlockSpec((None, 2, HD_A, HD_A, batch), lambda p: (layer, p, 0, 0, 0))],
        out_specs=[xspec, pl.BlockSpec((2, HD_A, HD_A, batch), lambda p: (p, 0, 0, 0))],
        scratch_shapes=[pltpu.VMEM((steps, LANES, batch), F32),
                        pltpu.VMEM((6, steps, LANES, batch), F32)],
        compiler_params=_params(("parallel",)),
        name="rwkv_scan_sample",
    )(r, k2, v, kkn, kka, ld, state_t)


def _sb_prompt_kernel(bias_ref, q0_ref, q1_ref, k0_ref, k1_ref, v0_ref, v1_ref, u_ref, o_ref,
                      kt_ref, vb_ref, qm_ref, acc_ref, carry_ref, before_ref, lb_ref, lkb_ref):
    i = pl.program_id(0)
    tq = SB_BLOCK
    scale = HD_B ** -0.5
    n_pairs = H_B // 2

    k_blk = jnp.concatenate([k0_ref[...], k1_ref[...]], axis=1)
    kt_ref[i] = k_blk.T.astype(BF16)
    v_blk = jnp.concatenate([v0_ref[...], v1_ref[...]], axis=1)
    q_blk = jnp.concatenate([q0_ref[...], q1_ref[...]], axis=1) * (-scale)
    lane = lax.broadcasted_iota(jnp.int32, (tq, LANES), 1)
    for p in range(n_pairs):
        vb_ref[i, p] = v_blk[:, p * LANES:(p + 1) * LANES].astype(BF16)
        qp = q_blk[:, p * LANES:(p + 1) * LANES]
        qm_ref[p, 0:tq] = jnp.where(lane < HD_B, qp, 0.0).astype(BF16)
        qm_ref[p, tq:2 * tq] = jnp.where(lane >= HD_B, qp, 0.0).astype(BF16)

    row = lax.broadcasted_iota(jnp.int32, (2 * tq, tq), 0)
    col = lax.broadcasted_iota(jnp.int32, (2 * tq, tq), 1)
    causal = col < jnp.where(row < tq, row, row - tq)
    first = lax.broadcasted_iota(jnp.int32, (2 * tq, 1), 0) < tq
    bias = [jnp.where(first, -bias_ref[2 * p], -bias_ref[2 * p + 1]) for p in range(n_pairs)]
    pairs = range(n_pairs)

    groups = [tuple(range(g, g + SB_GROUP)) for g in range(0, n_pairs, SB_GROUP)]

    def scores(j, diag, ps):
        zn = {p: _dot(qm_ref[p], kt_ref[j, p * LANES:(p + 1) * LANES, :]) + bias[p] for p in ps}
        lk = {p: _neg_softplus_of_neg(zn[p]) for p in ps}
        for p in ps:
            lb_ref[p] = lk[p] - zn[p]
        if diag:
            lk = {p: jnp.where(causal, lk[p], 0.0) for p in ps}
        for p in ps:
            lkb_ref[p] = lk[p].astype(BF16)
            before = jnp.zeros((2 * tq, LANES), F32) if diag else carry_ref[p]
            before_ref[p] = before
            carry_ref[p] = before + jnp.sum(lk[p], axis=-1, keepdims=True)

    def finish(j, diag, ps):
        within = {p: _dot(lkb_ref[p], u_ref[...]) for p in ps}
        for p in ps:
            att = jnp.exp(lb_ref[p] + within[p] + before_ref[p][:, 0:1])
            if diag:
                att = jnp.where(causal, att, 0.0)
            pv = _dot(att.astype(BF16), vb_ref[j, p])
            acc_ref[p] = pv if diag else acc_ref[p] + pv

    for ps in groups:
        scores(i, True, ps)

    @pl.when(i == 0)
    def _():
        for ps in groups:
            finish(0, True, ps)

    @pl.when(i > 0)
    def _():
        for ps in groups:
            finish(i, True, ps)
            scores(i - 1, False, ps)

        def trip(t, c):
            j = i - 1 - t
            for ps in groups:
                finish(j, False, ps)
                scores(j - 1, False, ps)
            return c

        lax.fori_loop(0, i - 1, trip, 0)
        for ps in groups:
            finish(0, False, ps)

    for p in pairs:
        o_ref[:, p * LANES:(p + 1) * LANES] = jnp.where(lane < HD_B, acc_ref[p, 0:tq], acc_ref[p, tq:2 * tq]).astype(
            o_ref.dtype)


def _sb_prompt(u, t, sb_bias):
    tq = SB_BLOCK
    nq = t // tq
    cb = RWKV_COLS // U_COL_BLOCK
    upper = (lax.broadcasted_iota(jnp.int32, (tq, tq), 0) > lax.broadcasted_iota(jnp.int32, (tq, tq), 1)).astype(BF16)
    uspec = lambda c: pl.BlockSpec((tq, U_COL_BLOCK), functools.partial(lambda i, c: (i, c), c=c))
    return pl.pallas_call(
        _sb_prompt_kernel,
        out_shape=jax.ShapeDtypeStruct((t, D_B), BF16),
        grid=(nq,),
        in_specs=[pl.BlockSpec(memory_space=pltpu.SMEM)] + [uspec(cb + c) for c in range(6)]
                 + [pl.BlockSpec((tq, tq), lambda i: (0, 0))],
        out_specs=pl.BlockSpec((tq, D_B), lambda i: (i, 0)),
        scratch_shapes=[pltpu.VMEM((nq, D_B, tq), BF16), pltpu.VMEM((nq, H_B // 2, tq, LANES), BF16),
                        pltpu.VMEM((H_B // 2, 2 * tq, LANES), BF16), pltpu.VMEM((H_B // 2, 2 * tq, LANES), F32),
                        pltpu.VMEM((H_B // 2, 2 * tq, LANES), F32), pltpu.VMEM((H_B // 2, 2 * tq, LANES), F32),
                        pltpu.VMEM((H_B // 2, 2 * tq, tq), F32), pltpu.VMEM((H_B // 2, 2 * tq, tq), BF16)],
        compiler_params=_params(("arbitrary",)),
        name="sb_prompt",
    )(sb_bias, u, u, u, u, u, u, upper)


Q_PAD = 8


def _sb_sample_kernel(pt_ref, bias_ref, qkv_ref, u_ref, *refs, n_pages, steps):
    k_refs = refs[:n_pages]
    v_refs = refs[n_pages:2 * n_pages]
    o_ref = refs[2 * n_pages]
    scale = HD_B ** -0.5
    rows = H_B * Q_PAD
    x = qkv_ref[0]
    q8 = x[:, 0:D_B] * scale
    zpad = jnp.zeros((PAGE_SIZE - Q_PAD, D_B), F32)
    k_new = jnp.concatenate([x[:, D_B:2 * D_B], zpad], axis=0).astype(BF16)
    v_new = jnp.concatenate([x[:, 2 * D_B:3 * D_B], zpad], axis=0).astype(BF16)
    lane_head = lax.broadcasted_iota(jnp.int32, (Q_PAD, D_B), 1) // HD_B
    qbd = jnp.concatenate([jnp.where(lane_head == h, q8, 0.0) for h in range(H_B)], axis=0).astype(BF16)
    bias = jnp.concatenate([jnp.full((Q_PAD, 1), bias_ref[h], F32) for h in range(H_B)], axis=0)

    page2d = lambda ref: ref[...].reshape(D_B, PAGE_SIZE).astype(BF16)
    z = jnp.concatenate([_dot(qbd, page2d(k_refs[p])) for p in range(n_pages)], axis=1) + bias
    zn = _dot_nt(qbd, k_new) + bias
    t_q = lax.broadcasted_iota(jnp.int32, (rows, PAGE_SIZE), 0) % Q_PAD
    c_k = lax.broadcasted_iota(jnp.int32, (rows, PAGE_SIZE), 1)
    new_ok = c_k < jnp.minimum(t_q, steps)

    lk = _neg_softplus(z)
    lkn_raw = _neg_softplus(zn)
    lkn = jnp.where(new_ok, lkn_raw, 0.0)
    upper = u_ref[...]
    later_n = _dot(lkn.astype(BF16), upper[:PAGE_SIZE, :PAGE_SIZE])
    att_n = jnp.where(new_ok, jnp.exp(zn + lkn_raw + later_n), 0.0)
    carry = jnp.sum(lkn, axis=-1, keepdims=True)
    blk = upper.shape[0]
    n_blk = n_pages * PAGE_SIZE // blk
    laters = [None] * n_blk
    for b in reversed(range(n_blk)):
        lkb = lk[:, b * blk:(b + 1) * blk]
        laters[b] = _dot(lkb.astype(BF16), upper) + carry
        carry = carry + jnp.sum(lkb, axis=-1, keepdims=True)
    att = jnp.exp(z + lk + jnp.concatenate(laters, axis=1)).astype(BF16)

    out = _dot(att_n.astype(BF16), v_new)
    for p in range(n_pages):
        out = out + _dot_nt(att[:, p * PAGE_SIZE:(p + 1) * PAGE_SIZE], page2d(v_refs[p]))
    y = jnp.zeros((Q_PAD, D_B), F32)
    for h in range(H_B):
        y = y + jnp.where(lane_head == h, out[h * Q_PAD:(h + 1) * Q_PAD], 0.0)
    o_ref[0] = y


def _sb_sample(qkv_pad, cache_kt, cache_vt, page_table, sb_bias, layer, steps):
    batch, n_pages = page_table.shape
    blk = MXU_DIM
    upper = (lax.broadcasted_iota(jnp.int32, (blk, blk), 0) > lax.broadcasted_iota(jnp.int32, (blk, blk), 1)).astype(BF16)
    page = lambda p: pl.BlockSpec((None, None, H_B, HD_B, PAGE_SIZE),
                                  functools.partial(lambda b, pt, p: (layer, pt[b, p], 0, 0, 0), p=p))
    grid_spec = pltpu.PrefetchScalarGridSpec(
        num_scalar_prefetch=1,
        grid=(batch,),
        in_specs=[pl.BlockSpec(memory_space=pltpu.SMEM),
                  pl.BlockSpec((1, Q_PAD, 3 * D_B), lambda b, pt: (b, 0, 0)),
                  pl.BlockSpec((blk, blk), lambda b, pt: (0, 0))]
                 + [page(p) for p in range(n_pages)] * 2,
        out_specs=pl.BlockSpec((1, Q_PAD, D_B), lambda b, pt: (b, 0, 0)),
    )
    return pl.pallas_call(
        functools.partial(_sb_sample_kernel, n_pages=n_pages, steps=steps),
        out_shape=jax.ShapeDtypeStruct((batch, Q_PAD, D_B), F32),
        grid_spec=grid_spec,
        compiler_params=_params(("parallel",)),
        name="sb_sample",
    )(page_table, sb_bias, qkv_pad, upper, *([cache_kt] * n_pages), *([cache_vt] * n_pages))


def _pool_mix(sums, x, cnts, wp_ref, scale_ref):
    outs = []
    for g in range(len(POOL_WINDOWS)):
        pooled = sums[g] / cnts[g] - x[:, g * C_POOL:(g + 1) * C_POOL]
        outs.append(_dot(pooled.astype(BF16), wp_ref[g].astype(BF16)))
    return jnp.concatenate(outs, axis=1) * scale_ref[...]


def _pool_prompt_kernel(u0_ref, u1_ref, wp_ref, scale_ref, o_ref, ext_ref):
    tp = u0_ref.shape[0]
    pad = 16
    i = pl.program_id(0)

    @pl.when(i == 0)
    def _():
        ext_ref[0:pad, :] = jnp.zeros((pad, D_C), F32)

    x = jnp.concatenate([u0_ref[...], u1_ref[...]], axis=1)
    ext_ref[pad:pad + tp, :] = x
    pos = i * tp + lax.broadcasted_iota(jnp.int32, (tp, 1), 0)
    sums, cnts = [], []
    for g, w in enumerate(POOL_WINDOWS):
        s = x[:, g * C_POOL:(g + 1) * C_POOL]
        for back in range(1, w):
            s = s + ext_ref[pad - back:pad - back + tp, g * C_POOL:(g + 1) * C_POOL]
        sums.append(s)
        cnts.append(jnp.minimum(w, pos + 1).astype(F32))
    o_ref[...] = _pool_mix(sums, x, cnts, wp_ref, scale_ref).astype(o_ref.dtype)
    ext_ref[0:pad, :] = x[tp - pad:, :]


def _pool_prompt(u, t, w_pool, pool_scale, tp):
    cb = (RWKV_COLS + 3 * D_B) // U_COL_BLOCK
    uspec = lambda c: pl.BlockSpec((tp, U_COL_BLOCK), functools.partial(lambda i, c: (i, c), c=c))
    return pl.pallas_call(
        _pool_prompt_kernel,
        out_shape=jax.ShapeDtypeStruct((t, D_C), BF16),
        grid=(t // tp,),
        in_specs=[uspec(cb), uspec(cb + 1),
                  pl.BlockSpec((len(POOL_WINDOWS), C_POOL, C_POOL), lambda i: (0, 0, 0)),
                  pl.BlockSpec((1, D_C), lambda i: (0, 0))],
        out_specs=pl.BlockSpec((tp, D_C), lambda i: (i, 0)),
        scratch_shapes=[pltpu.VMEM((16 + tp, D_C), F32)],
        compiler_params=_params(("arbitrary",)),
        name="pool_prompt",
    )(u, u, w_pool, pool_scale.reshape(1, D_C))


def _pool_sample_kernel(u0_ref, u1_ref, pre_ref, wp_ref, scale_ref, o_ref, *, steps, batch, start_pos):
    x_all = jnp.concatenate([u0_ref[...], u1_ref[...]], axis=1)
    ext = [pre_ref[r] for r in range(POOL_BUF)] + [x_all[t * batch:(t + 1) * batch] for t in range(steps)]
    for t in range(steps):
        sums, cnts = [], []
        for g, w in enumerate(POOL_WINDOWS):
            s = ext[POOL_BUF + t][:, g * C_POOL:(g + 1) * C_POOL]
            for back in range(1, w):
                s = s + ext[POOL_BUF + t - back][:, g * C_POOL:(g + 1) * C_POOL]
            sums.append(s)
            cnts.append(float(min(w, start_pos + t + 1)))
        o_ref[t * batch:(t + 1) * batch, :] = _pool_mix(sums, ext[POOL_BUF + t], cnts, wp_ref, scale_ref).astype(
            o_ref.dtype)


def _pool_sample(u, row_block0, prefix_tm, w_pool, pool_scale, steps, batch, start_pos):
    rows = steps * batch
    cb = (RWKV_COLS + 3 * D_B) // U_COL_BLOCK
    uspec = lambda c: pl.BlockSpec((rows, U_COL_BLOCK), functools.partial(lambda i, c: (row_block0, c), c=c))
    return pl.pallas_call(
        functools.partial(_pool_sample_kernel, steps=steps, batch=batch, start_pos=start_pos),
        out_shape=jax.ShapeDtypeStruct((rows, D_C), BF16),
        grid=(1,),
        in_specs=[uspec(cb), uspec(cb + 1),
                  pl.BlockSpec((POOL_BUF, batch, D_C), lambda i: (0, 0, 0)),
                  pl.BlockSpec((len(POOL_WINDOWS), C_POOL, C_POOL), lambda i: (0, 0, 0)),
                  pl.BlockSpec((1, D_C), lambda i: (0, 0))],
        out_specs=pl.BlockSpec((rows, D_C), lambda i: (0, 0)),
        compiler_params=_params(("arbitrary",)),
        name="pool_sample",
    )(u, u, prefix_tm, w_pool, pool_scale.reshape(1, D_C))


DENSE_SUB_ROWS = 272
TM_DENSE = 1088
TM_WIDE = 2176
TM_DOWN = 544


def kernel(x_prompt, x_sample, p_prompt, p_sample, cache_k, cache_v, page_table, state_wkv, state_shift, state_pool, g_mix, w_in, mu_shift, w0, w_lora_up, a0, a_lora_up, g_lora_up, k_k, k_a, r_k, ln_w, ln_b, sb_bias, w_pool, pool_scale, w_o, g_ffn, w_gate, w_up, w_down, g_ple, w_pg, w_pp, g_final):
    depth = w_in.shape[0]
    tp = x_prompt.shape[1]
    bs, ts = x_sample.shape[0], x_sample.shape[1]
    rows_s = bs * ts
    m = tp + rows_s
    n_pages = page_table.shape[1]
    past_len = n_pages * PAGE_SIZE

    to_tm = lambda x: jnp.swapaxes(x, 0, 1).reshape(rows_s, x.shape[-1])
    from_tm = lambda x: jnp.swapaxes(x.reshape(ts, bs, x.shape[-1]), 0, 1)

    h = jnp.concatenate([x_prompt[0], to_tm(x_sample)], axis=0)
    cache_kt = jnp.transpose(cache_k, (0, 1, 3, 4, 2))
    cache_vt = jnp.transpose(cache_v, (0, 1, 3, 4, 2))
    state_t = jnp.transpose(state_wkv, (0, 2, 3, 4, 1))

    outs = {n: [] for n in ("kp", "vp", "ks", "vs", "wp", "ws", "sp", "ss", "pp", "ps")}
    for i in range(depth):
        lw = dict(mu_shift=mu_shift[i], w0=w0[i], w_lora_up=w_lora_up[i], a0=a0[i], a_lora_up=a_lora_up[i],
                  g_lora_up=g_lora_up[i], k_k=k_k[i], k_a=k_a[i], r_k=r_k[i].reshape(-1), ln_w=ln_w[i], ln_b=ln_b[i])
        xn = _rmsnorm(h, g_mix[i], BF16, TM_DENSE)
        u = _dense([(xn, w_in, D_MODEL, 0, 0)], i, 1, _epi_plain, None, TM_WIDE, 768, F32, "dense_in")

        prep_p = _rwkv_prep(u, 0, tp, jnp.zeros((1, RWKV_COLS), F32), 1, 256, lw)
        ya_p, wkv_p = _rwkv_scan_prompt(*prep_p[:6], prep_p[7], prep_p[6], lw["ln_w"], lw["ln_b"])
        prep_s = _rwkv_prep(u, tp // rows_s, rows_s, state_shift[i], bs, rows_s, lw)
        y_s, wkv_s = _rwkv_scan_sample(*prep_s[:6], state_t, i, ts, bs)
        ya_s = _rwkv_post(y_s, prep_s[7], prep_s[6], lw, rows_s)

        yb_p = _sb_prompt(u, tp, sb_bias[i])
        qkv_s = from_tm(u[tp:, RWKV_COLS:RWKV_COLS + 3 * D_B])
        qkv_pad = jnp.pad(qkv_s, ((0, 0), (0, Q_PAD - ts), (0, 0)))
        yb_s = _sb_sample(qkv_pad, cache_kt, cache_vt, page_table, sb_bias[i], i, ts)
        yb_s = to_tm(yb_s[:, :ts]).astype(BF16)

        yc_p = _pool_prompt(u, tp, w_pool[i], pool_scale[i], 512)
        yc_s = _pool_sample(u, tp // rows_s, jnp.swapaxes(state_pool[i], 0, 1), w_pool[i], pool_scale[i],
                            ts, bs, past_len)

        ya = jnp.concatenate([ya_p, ya_s], axis=0)
        yb = jnp.concatenate([yb_p, yb_s], axis=0)
        yc = jnp.concatenate([yc_p, yc_s], axis=0)
        h = _dense([(ya, w_o, D_A, 0, 0), (yb, w_o, D_B, D_A // D_B, 0),
                    (yc, w_o, D_C, (D_A + D_B) // D_C, 0)], i, 1, _epi_residual, h, TM_WIDE, 512, F32, "dense_o")

        xn = _rmsnorm(h, g_ffn[i], BF16, TM_DENSE)
        act = _dense([(xn, w_gate, D_MODEL, 0, 0), (xn, w_up, D_MODEL, 0, 1)], i, 2, _epi_swiglu, None,
                     TM_WIDE, 512, BF16, "dense_ffn_in")
        h = _dense([(act, w_down, D_FF, 0, 0)], i, 1, _epi_residual, h, TM_DOWN, 512, F32, "dense_ffn_out")

        xn = _rmsnorm(h, g_ple[i], BF16, TM_DENSE)
        p_i = jnp.concatenate([p_prompt[i, 0], to_tm(p_sample[i])], axis=0).astype(BF16)
        h = _dense([(xn, w_pg, D_MODEL, 0, 0), (p_i, w_pp, D_PLE, 0, 1)], i, 2, _epi_gated_residual, h,
                   TM_WIDE, 512, F32, "dense_ple")

        kv_p = u[:tp, RWKV_COLS + D_B:RWKV_COLS + 3 * D_B]
        outs["kp"].append(kv_p[:, :D_B].reshape(1, tp, H_B, HD_B))
        outs["vp"].append(kv_p[:, D_B:].reshape(1, tp, H_B, HD_B))
        outs["ks"].append(qkv_s[..., D_B:2 * D_B].reshape(bs, ts, H_B, HD_B))
        outs["vs"].append(qkv_s[..., 2 * D_B:].reshape(bs, ts, H_B, HD_B))
        outs["wp"].append(wkv_p.reshape(1, H_A, HD_A, HD_A))
        outs["ws"].append(jnp.transpose(wkv_s, (3, 0, 1, 2)))
        outs["sp"].append(u[tp - 1:tp, :RWKV_COLS])
        outs["ss"].append(u[m - bs:, :RWKV_COLS])
        uc_s = from_tm(u[tp:, RWKV_COLS + 3 * D_B:])
        outs["pp"].append(u[tp - POOL_BUF:tp, RWKV_COLS + 3 * D_B:][None])
        outs["ps"].append(jnp.concatenate([state_pool[i][:, ts:], uc_s], axis=1))

    y_prompt = _rmsnorm(h, g_final, F32, rows_s, 0, tp)[None]
    y_sample = from_tm(_rmsnorm(h, g_final, F32, rows_s, tp // rows_s, rows_s))
    stack = lambda n: jnp.stack(outs[n])
    return (y_prompt, y_sample, stack("kp"), stack("vp"), stack("ks"), stack("vs"), stack("wp"), stack("ws"),
            stack("sp"), stack("ss"), stack("pp"), stack("ps"))
```

```python
import functools

import jax
import jax.numpy as jnp
from jax import lax
from jax.experimental import pallas as pl
from jax.experimental.pallas import tpu as pltpu

F32 = jnp.float32
BF16 = jnp.bfloat16

D_MODEL = 2048
D_A = 1024
HD_A = 64
H_A = 16
LORA_W = 64
LORA_A = 64
LORA_G = 128
D_B = 512
HD_B = 64
H_B = 8
D_C = 512
POOL_WINDOWS = (2, 4, 8, 16)
C_POOL = 128
POOL_BUF = 15
RWKV_COLS = 3 * D_A + LORA_W + LORA_A + LORA_G
IN_COLS = RWKV_COLS + 3 * D_B + D_C
D_FF = 5632
D_PLE = 256
PAGE_SIZE = 128
EPS = 1e-6
GN_EPS = 64e-5

LANES = 128
MXU_DIM = 256
RWKV_CHUNK = 64
RWKV_CHUNKS_PER_STEP = 4
SB_BLOCK = 256
SB_GROUP = 1
U_COL_BLOCK = 256
VMEM_LIMIT_MB = 56
NEG_LOG2E = -1.4426950408889634


def _params(sem, vmem_mb=VMEM_LIMIT_MB):
    return pltpu.CompilerParams(dimension_semantics=sem, vmem_limit_bytes=vmem_mb * 2**20)


def _dot(a, b):
    return jnp.dot(a, b, preferred_element_type=F32)


def _dot_nt(a, b):
    return lax.dot_general(a, b, (((1,), (1,)), ((), ())), preferred_element_type=F32)


def _split(x):
    hi = x.astype(BF16)
    lo = (x - hi.astype(F32)).astype(BF16)
    return hi, lo


def _dot_left_exact(m_bf16, x):
    hi, lo = _split(x)
    return _dot(m_bf16, hi) + _dot(m_bf16, lo)


def _dot_right_exact(x, m_bf16):
    hi, lo = _split(x)
    return _dot(hi, m_bf16) + _dot(lo, m_bf16)


def _neg_softplus(z):
    return _neg_softplus_of_neg(-z)


def _neg_softplus_of_neg(zn):
    e = jnp.exp2(jnp.abs(zn) * NEG_LOG2E)
    return jnp.minimum(zn, 0.0) - jnp.log(1.0 + e)


def _sigmoid(x):
    return 1.0 / (1.0 + jnp.exp(-x))


def _rmsnorm_kernel(h_ref, g_ref, o_ref):
    x = h_ref[...]
    y = x * lax.rsqrt(jnp.mean(x * x, axis=-1, keepdims=True) + EPS)
    o_ref[...] = (y * g_ref[...]).astype(o_ref.dtype)


def _rmsnorm(h, g, out_dtype, tm, row_block0=0, rows=None):
    d = h.shape[1]
    m = h.shape[0] if rows is None else rows
    return pl.pallas_call(
        _rmsnorm_kernel,
        out_shape=jax.ShapeDtypeStruct((m, d), out_dtype),
        grid=(m // tm,),
        in_specs=[pl.BlockSpec((tm, d), lambda i: (i + row_block0, 0)), pl.BlockSpec((1, d), lambda i: (0, 0))],
        out_specs=pl.BlockSpec((tm, d), lambda i: (i, 0)),
        compiler_params=_params(("parallel",)),
        name="rmsnorm",
    )(h, g.reshape(1, d))


def _dense_kernel(*refs, n_terms, a_of, acc_of, n_acc, has_res, epilogue):
    n_a = max(a_of) + 1
    a_refs = [refs[a_of[t]] for t in range(n_terms)]
    w_refs = refs[n_a:n_a + n_terms]
    pos = n_a + n_terms
    res_ref = refs[pos] if has_res else None
    pos += int(has_res)
    o_ref = refs[pos]
    wbf_refs = refs[pos + 1:]

    @pl.when(pl.program_id(1) == 0)
    def _():
        for t in range(n_terms):
            wbf_refs[t][...] = w_refs[t][...].astype(BF16)

    tm = o_ref.shape[0]
    for r0 in range(0, tm, DENSE_SUB_ROWS):
        rows = slice(r0, r0 + DENSE_SUB_ROWS)
        accs = [None] * n_acc
        for t in range(n_terms):
            d = _dot(a_refs[t][rows, :], wbf_refs[t][...])
            accs[acc_of[t]] = d if accs[acc_of[t]] is None else accs[acc_of[t]] + d
        res = res_ref[rows, :] if has_res else None
        o_ref[rows, :] = epilogue(accs, res).astype(o_ref.dtype)


def _dense(terms, layer, n_acc, epilogue, res, tm, tn, out_dtype, name):
    m = terms[0][0].shape[0]
    n = terms[0][1].shape[2]
    n_terms = len(terms)
    acts = []
    a_of = []
    for t in terms:
        if not any(t[0] is a for a in acts):
            acts.append(t[0])
        a_of.append([t[0] is a for a in acts].index(True))
    in_specs = [pl.BlockSpec((tm, a.shape[1]), lambda j, i: (i, 0)) for a in acts]
    in_specs += [pl.BlockSpec((None, t[2], tn), functools.partial(lambda j, i, r: (layer, r, j), r=t[3]))
                 for t in terms]
    args = acts + [t[1] for t in terms]
    if res is not None:
        in_specs.append(pl.BlockSpec((tm, tn), lambda j, i: (i, j)))
        args.append(res)
    kern = functools.partial(_dense_kernel, n_terms=n_terms, a_of=a_of, acc_of=[t[4] for t in terms], n_acc=n_acc,
                             has_res=res is not None, epilogue=epilogue)
    return pl.pallas_call(
        kern,
        out_shape=jax.ShapeDtypeStruct((m, n), out_dtype),
        grid=(n // tn, m // tm),
        in_specs=in_specs,
        out_specs=pl.BlockSpec((tm, tn), lambda j, i: (i, j)),
        scratch_shapes=[pltpu.VMEM((t[2], tn), BF16) for t in terms],
        compiler_params=_params(("arbitrary", "arbitrary")),
        name=name,
    )(*args)


def _epi_plain(accs, res):
    return accs[0]


def _epi_residual(accs, res):
    return res + accs[0]


def _epi_swiglu(accs, res):
    gate, up = accs
    return gate * _sigmoid(gate) * up


def _epi_gated_residual(accs, res):
    return res + _sigmoid(accs[0]) * accs[1]


def _seg_mats():
    row_head = lax.broadcasted_iota(jnp.int32, (D_A, LANES), 0) // HD_A
    col = lax.broadcasted_iota(jnp.int32, (D_A, LANES), 1)
    e1 = (row_head == col).astype(BF16)
    return e1, e1.T


def _segsum(x, e1, e2):
    return _dot_right_exact(_dot_right_exact(x, e1), e2)


def _rwkv_prep_kernel(u_ref, prev_ref, mu_ref, w0_ref, a0_ref, kk_ref, ka_ref, rk_ref,
                      wl_ref, al_ref, gl_ref, e1_ref, e2_ref,
                      r_o, k_o, v_o, kkn_o, kka_o, ld_o, g_o, bonus_o, carry_ref, *, lag):
    rows = u_ref.shape[0]

    @pl.when(pl.program_id(0) == 0)
    def _():
        carry_ref[...] = prev_ref[...]

    u = u_ref[...]
    if lag == 1:
        rolled = pltpu.roll(u, 1, axis=0)
        row = lax.broadcasted_iota(jnp.int32, u.shape, 0)
        shifted = jnp.where(row == 0, carry_ref[...], rolled)
        carry_ref[...] = u[rows - 1:rows, :]
    else:
        shifted = jnp.concatenate([carry_ref[...], u[:rows - lag, :]], axis=0)
        carry_ref[...] = u[rows - lag:, :]
    xs = u + (shifted - u) * mu_ref[...]

    r = xs[:, 0:D_A]
    k = xs[:, D_A:2 * D_A]
    v = xs[:, 2 * D_A:3 * D_A]
    x_wa = xs[:, 3 * D_A:3 * D_A + LORA_W + LORA_A]
    xg = xs[:, 3 * D_A + LORA_W + LORA_A:]
    lane = lax.broadcasted_iota(jnp.int32, x_wa.shape, 1)
    tw = jnp.where(lane < LORA_W, jnp.tanh(x_wa), 0.0).astype(BF16)
    xa = jnp.where(lane >= LORA_W, x_wa, 0.0).astype(BF16)
    w_lin = w0_ref[...] + _dot(tw, wl_ref[...].astype(BF16))
    w = _neg_softplus(-w_lin) - 0.5
    ld_o[...] = -jnp.exp(w)
    a = _sigmoid(a0_ref[...] + _dot(xa, al_ref[...].astype(BF16)))
    g_o[...] = _dot(_sigmoid(xg).astype(BF16), gl_ref[...].astype(BF16))

    e1 = e1_ref[...]
    e2 = e2_ref[...]
    kk = k * kk_ref[...]
    norm = jnp.maximum(jnp.sqrt(_segsum(kk * kk, e1, e2)), 1e-12)
    kkn = kk / norm
    k2 = k * (1.0 + (a - 1.0) * ka_ref[...])
    r_o[...] = r
    k_o[...] = k2
    v_o[...] = v
    kkn_o[...] = kkn
    kka_o[...] = kkn * a
    bonus_o[...] = _segsum(r * k2 * rk_ref[...], e1, e2) * v


def _rwkv_prep(u, row_block0, rows, prev, lag, tr, lw):
    e1, e2 = _seg_mats()
    zeros_w = jnp.zeros((LORA_W, D_A), F32)
    wl = jnp.concatenate([lw["w_lora_up"], zeros_w], axis=0)
    al = jnp.concatenate([zeros_w, lw["a_lora_up"]], axis=0)
    vec = lambda x: x.reshape(1, -1)
    const = lambda shape: pl.BlockSpec(shape, lambda i: (0, 0))
    out = jax.ShapeDtypeStruct((rows, D_A), F32)
    ospec = pl.BlockSpec((tr, D_A), lambda i: (i, 0))
    return pl.pallas_call(
        functools.partial(_rwkv_prep_kernel, lag=lag),
        out_shape=[out] * 8,
        grid=(rows // tr,),
        in_specs=[pl.BlockSpec((tr, RWKV_COLS), lambda i: (i + row_block0, 0)),
                  const((lag, RWKV_COLS)), const((1, RWKV_COLS)),
                  const((1, D_A)), const((1, D_A)), const((1, D_A)), const((1, D_A)), const((1, D_A)),
                  const((LORA_W + LORA_A, D_A)), const((LORA_W + LORA_A, D_A)), const((LORA_G, D_A)),
                  const((D_A, LANES)), const((LANES, D_A))],
        out_specs=[ospec] * 8,
        scratch_shapes=[pltpu.VMEM((lag, RWKV_COLS), F32)],
        compiler_params=_params(("arbitrary",)),
        name="rwkv_prep",
    )(u, prev, vec(lw["mu_shift"]), vec(lw["w0"]), vec(lw["a0"]), vec(lw["k_k"]), vec(lw["k_a"]),
      vec(lw["r_k"]), wl, al, lw["g_lora_up"], e1, e2)


def _rwkv_post_kernel(y_ref, bonus_ref, g_ref, lnw_ref, lnb_ref, e1_ref, e2_ref, o_ref):
    e1 = e1_ref[...]
    e2 = e2_ref[...]
    y = y_ref[...]
    mean = _segsum(y, e1, e2) * (1.0 / HD_A)
    d = y - mean
    var = _segsum(d * d, e1, e2) * (1.0 / HD_A)
    yn = d * lax.rsqrt(var + GN_EPS) * lnw_ref[...] + lnb_ref[...]
    o_ref[...] = ((yn + bonus_ref[...]) * g_ref[...]).astype(o_ref.dtype)


def _rwkv_post(y, bonus, g, lw, tr):
    rows = y.shape[0]
    e1, e2 = _seg_mats()
    rspec = pl.BlockSpec((tr, D_A), lambda i: (i, 0))
    const = lambda shape: pl.BlockSpec(shape, lambda i: (0, 0))
    return pl.pallas_call(
        _rwkv_post_kernel,
        out_shape=jax.ShapeDtypeStruct((rows, D_A), BF16),
        grid=(rows // tr,),
        in_specs=[rspec, rspec, rspec, const((1, D_A)), const((1, D_A)), const((D_A, LANES)), const((LANES, D_A))],
        out_specs=rspec,
        compiler_params=_params(("parallel",)),
        name="rwkv_post",
    )(y, bonus, g, lw["ln_w"].reshape(1, D_A), lw["ln_b"].reshape(1, D_A), e1, e2)


def _unit_lower_inverses(ls, n):
    row = lax.broadcasted_iota(jnp.int32, (n, n), 0)
    col = lax.broadcasted_iota(jnp.int32, (n, n), 1)
    eye = (row == col).astype(F32)
    ts = [eye + l for l in ls]
    ps = [l.astype(BF16) for l in ls]
    n_factors = n.bit_length() - 1
    for stage in range(1, n_factors):
        ps = [_dot(p, p).astype(BF16) for p in ps]
        ts = [t + _dot(t.astype(BF16), p) for t, p in zip(ts, ps)]
    return ts


def _rwkv_scan_kernel(r_ref, k_ref, v_ref, kkn_ref, kka_ref, ld_ref, bonus_ref, g_ref, lnw_ref, lnb_ref, tri_ref,
                      y_ref, s_ref):
    c = RWKV_CHUNK
    n_sub = RWKV_CHUNKS_PER_STEP
    heads = range(H_A)
    sl = [slice(h * HD_A, (h + 1) * HD_A) for h in heads]

    @pl.when(pl.program_id(0) == 0)
    def _():
        s_ref[...] = jnp.zeros_like(s_ref)

    ld_all = ld_ref[...]
    cum_all = _dot_left_exact(tri_ref[...], ld_all)
    row = lax.broadcasted_iota(jnp.int32, (2 * c, 2 * c), 0)
    col = lax.broadcasted_iota(jnp.int32, (2 * c, 2 * c), 1)
    keep = col % c < row % c + jnp.where(row < c, 0, 1)
    zeros_cv = jnp.zeros((c, HD_A), F32)

    pre = []
    for j in range(n_sub):
        rows = slice(j * c, (j + 1) * c)
        ld, cum = ld_all[rows], cum_all[rows]
        last = cum[c - 1:c, :]
        p_inv = jnp.exp(-cum)
        p_end = jnp.exp(last - cum)
        kkn, kka, k2, v = kkn_ref[rows, :], kka_ref[rows, :], k_ref[rows, :], v_ref[rows, :]
        a_t = -kkn * jnp.exp(cum - ld)
        r_t = r_ref[rows, :] * jnp.exp(cum)
        b_t, k_t, b_e, k_e = kka * p_inv, k2 * p_inv, kka * p_end, k2 * p_end
        ar = [jnp.concatenate([a_t[:, sl[h]], r_t[:, sl[h]]], axis=0).astype(BF16) for h in heads]
        bk = [jnp.concatenate([b_t[:, sl[h]], k_t[:, sl[h]]], axis=0).astype(BF16) for h in heads]
        be = [jnp.concatenate([b_e[:, sl[h]], k_e[:, sl[h]]], axis=0).astype(BF16) for h in heads]
        pre.append(dict(ar=ar, bk=bk, be=be, vh=[v[:, sl[h]] for h in heads], p_tot=jnp.exp(last)))
    for p in pre:
        p["gm"] = [jnp.where(keep, _dot_nt(p["ar"][h], p["bk"][h]), 0.0) for h in heads]
        p["gmb"] = [g.astype(BF16) for g in p["gm"]]
    for p in pre:
        p["lakv"] = [_dot(p["gmb"][h][:c, :], jnp.concatenate([zeros_cv, p["vh"][h]], axis=0).astype(BF16))
                     for h in heads]
    tinv = _unit_lower_inverses([p["gm"][h][:c, :c] for p in pre for h in heads], c)
    tinv = [t.astype(BF16) for t in tinv]

    s = [s_ref[h] for h in heads]
    for j, p in enumerate(pre):
        ars = [_dot_nt(p["ar"][h], s[h].astype(BF16)) for h in heads]
        x = [ars[h][:c] + p["lakv"][h] for h in heads]
        w = [_dot(tinv[j * H_A + h], x[h].astype(BF16)) for h in heads]
        wv = [jnp.concatenate([w[h], p["vh"][h]], axis=0) for h in heads]
        ys = [ars[h][c:] + _dot(p["gmb"][h][c:, :], wv[h].astype(BF16)) for h in heads]
        s = [s[h] * p["p_tot"][:, sl[h]] + _dot(wv[h].T.astype(BF16), p["be"][h]) for h in heads]
        mean = [jnp.mean(y, axis=-1, keepdims=True) for y in ys]
        dev = [ys[h] - mean[h] for h in heads]
        var = [jnp.mean(d * d, axis=-1, keepdims=True) for d in dev]
        yn = jnp.concatenate([dev[h] * lax.rsqrt(var[h] + GN_EPS) for h in heads], axis=1)
        rows = slice(j * c, (j + 1) * c)
        out = (yn * lnw_ref[...] + lnb_ref[...] + bonus_ref[rows, :]) * g_ref[rows, :]
        y_ref[rows, :] = out.astype(y_ref.dtype)
    for h in heads:
        s_ref[h] = s[h]


def _rwkv_scan_prompt(r, k2, v, kkn, kka, ld, bonus, g, ln_w, ln_b):
    t = r.shape[0]
    c = RWKV_CHUNK * RWKV_CHUNKS_PER_STEP
    ri = lax.broadcasted_iota(jnp.int32, (c, c), 0)
    ci = lax.broadcasted_iota(jnp.int32, (c, c), 1)
    tri = ((ri >= ci) & (ri // RWKV_CHUNK == ci // RWKV_CHUNK)).astype(BF16)
    rspec = pl.BlockSpec((c, D_A), lambda i: (i, 0))
    return pl.pallas_call(
        _rwkv_scan_kernel,
        out_shape=[jax.ShapeDtypeStruct((t, D_A), BF16), jax.ShapeDtypeStruct((H_A, HD_A, HD_A), F32)],
        grid=(t // c,),
        in_specs=[rspec] * 8 + [pl.BlockSpec((1, D_A), lambda i: (0, 0))] * 2 + [pl.BlockSpec((c, c), lambda i: (0, 0))],
        out_specs=[rspec, pl.BlockSpec((H_A, HD_A, HD_A), lambda i: (0, 0, 0))],
        compiler_params=_params(("arbitrary",)),
        name="rwkv_scan_prompt",
    )(r, k2, v, kkn, kka, ld, bonus, g, ln_w.reshape(1, D_A), ln_b.reshape(1, D_A), tri)


V_CHUNK = 8


def _rwkv_scan_sample_kernel(r_ref, k_ref, v_ref, kkn_ref, kka_ref, ld_ref, s_in_ref, y_ref, s_out_ref,
                             yt_ref, xt_ref, *, steps, batch):
    for q, ref in enumerate((r_ref, k_ref, v_ref, kkn_ref, kka_ref, ld_ref)):
        for t in range(steps):
            x = ref[t * batch:(t + 1) * batch, :]
            if q == 5:
                x = jnp.exp(x)
            xt_ref[q, t] = x.T

    def chunk(i, carry):
        hh = i // (HD_A // V_CHUNK)
        v0 = pl.multiple_of((i % (HD_A // V_CHUNK)) * V_CHUNK, V_CHUNK)
        base = pl.multiple_of(hh * HD_A, HD_A)
        s = s_in_ref[hh, pl.ds(v0, V_CHUNK)]
        for t in range(steps):
            rt = xt_ref[0, t, pl.ds(base, HD_A), :]
            kt = xt_ref[1, t, pl.ds(base, HD_A), :]
            vt = xt_ref[2, t, pl.ds(base + v0, V_CHUNK), :]
            kkt = xt_ref[3, t, pl.ds(base, HD_A), :]
            kat = xt_ref[4, t, pl.ds(base, HD_A), :]
            dt = xt_ref[5, t, pl.ds(base, HD_A), :]
            sa = jnp.sum(s * kkt[None], axis=1)
            s = s * dt[None] - sa[:, None, :] * kat[None] + vt[:, None, :] * kt[None]
            yt_ref[t, pl.ds(base + v0, V_CHUNK), :] = jnp.sum(s * rt[None], axis=1)
        s_out_ref[hh, pl.ds(v0, V_CHUNK)] = s
        return carry

    lax.fori_loop(0, 2 * HD_A // V_CHUNK, chunk, 0)

    for t in range(steps):
        y_ref[t * batch:(t + 1) * batch, :] = yt_ref[t].T


def _rwkv_scan_sample(r, k2, v, kkn, kka, ld, state_t, layer, steps, batch):
    rows = r.shape[0]
    xspec = pl.BlockSpec((rows, LANES), lambda p: (0, p))
    return pl.pallas_call(
        functools.partial(_rwkv_scan_sample_kernel, steps=steps, batch=batch),
        out_shape=[jax.ShapeDtypeStruct((rows, D_A), F32),
                   jax.ShapeDtypeStruct((H_A, HD_A, HD_A, batch), F32)],
        grid=(H_A // 2,),
        in_specs=[xspec] * 6 + [pl.BlockSpec((None, 2, HD_A, HD_A, batch), lambda p: (layer, p, 0, 0, 0))],
        out_specs=[xspec, pl.BlockSpec((2, HD_A, HD_A, batch), lambda p: (p, 0, 0, 0))],
        scratch_shapes=[pltpu.VMEM((steps, LANES, batch), F32),
                        pltpu.VMEM((6, steps, LANES, batch), F32)],
        compiler_params=_params(("parallel",)),
        name="rwkv_scan_sample",
    )(r, k2, v, kkn, kka, ld, state_t)


def _sb_prompt_kernel(bias_ref, q0_ref, q1_ref, k0_ref, k1_ref, v0_ref, v1_ref, u_ref, o_ref,
                      kt_ref, vb_ref, qm_ref, acc_ref, carry_ref, before_ref, lb_ref, lkb_ref):
    i = pl.program_id(0)
    tq = SB_BLOCK
    scale = HD_B ** -0.5
    n_pairs = H_B // 2

    k_blk = jnp.concatenate([k0_ref[...], k1_ref[...]], axis=1)
    kt_ref[i] = k_blk.T.astype(BF16)
    v_blk = jnp.concatenate([v0_ref[...], v1_ref[...]], axis=1)
    q_blk = jnp.concatenate([q0_ref[...], q1_ref[...]], axis=1) * (-scale)
    lane = lax.broadcasted_iota(jnp.int32, (tq, LANES), 1)
    for p in range(n_pairs):
        vb_ref[i, p] = v_blk[:, p * LANES:(p + 1) * LANES].astype(BF16)
        qp = q_blk[:, p * LANES:(p + 1) * LANES]
        qm_ref[p, 0:tq] = jnp.where(lane < HD_B, qp, 0.0).astype(BF16)
        qm_ref[p, tq:2 * tq] = jnp.where(lane >= HD_B, qp, 0.0).astype(BF16)

    row = lax.broadcasted_iota(jnp.int32, (2 * tq, tq), 0)
    col = lax.broadcasted_iota(jnp.int32, (2 * tq, tq), 1)
    causal = col < jnp.where(row < tq, row, row - tq)
    first = lax.broadcasted_iota(jnp.int32, (2 * tq, 1), 0) < tq
    bias = [jnp.where(first, -bias_ref[2 * p], -bias_ref[2 * p + 1]) for p in range(n_pairs)]
    pairs = range(n_pairs)

    groups = [tuple(range(g, g + SB_GROUP)) for g in range(0, n_pairs, SB_GROUP)]

    def scores(j, diag, ps):
        zn = {p: _dot(qm_ref[p], kt_ref[j, p * LANES:(p + 1) * LANES, :]) + bias[p] for p in ps}
        lk = {p: _neg_softplus_of_neg(zn[p]) for p in ps}
        for p in ps:
            lb_ref[p] = lk[p] - zn[p]
        if diag:
            lk = {p: jnp.where(causal, lk[p], 0.0) for p in ps}
        for p in ps:
            lkb_ref[p] = lk[p].astype(BF16)
            before = jnp.zeros((2 * tq, LANES), F32) if diag else carry_ref[p]
            before_ref[p] = before
            carry_ref[p] = before + jnp.sum(lk[p], axis=-1, keepdims=True)

    def finish(j, diag, ps):
        within = {p: _dot(lkb_ref[p], u_ref[...]) for p in ps}
        for p in ps:
            att = jnp.exp(lb_ref[p] + within[p] + before_ref[p][:, 0:1])
            if diag:
                att = jnp.where(causal, att, 0.0)
            pv = _dot(att.astype(BF16), vb_ref[j, p])
            acc_ref[p] = pv if diag else acc_ref[p] + pv

    for ps in groups:
        scores(i, True, ps)

    @pl.when(i == 0)
    def _():
        for ps in groups:
            finish(0, True, ps)

    @pl.when(i > 0)
    def _():
        for ps in groups:
            finish(i, True, ps)
            scores(i - 1, False, ps)

        def trip(t, c):
            j = i - 1 - t
            for ps in groups:
                finish(j, False, ps)
                scores(j - 1, False, ps)
            return c

        lax.fori_loop(0, i - 1, trip, 0)
        for ps in groups:
            finish(0, False, ps)

    for p in pairs:
        o_ref[:, p * LANES:(p + 1) * LANES] = jnp.where(lane < HD_B, acc_ref[p, 0:tq], acc_ref[p, tq:2 * tq]).astype(
            o_ref.dtype)


def _sb_prompt(u, t, sb_bias):
    tq = SB_BLOCK
    nq = t // tq
    cb = RWKV_COLS // U_COL_BLOCK
    upper = (lax.broadcasted_iota(jnp.int32, (tq, tq), 0) > lax.broadcasted_iota(jnp.int32, (tq, tq), 1)).astype(BF16)
    uspec = lambda c: pl.BlockSpec((tq, U_COL_BLOCK), functools.partial(lambda i, c: (i, c), c=c))
    return pl.pallas_call(
        _sb_prompt_kernel,
        out_shape=jax.ShapeDtypeStruct((t, D_B), BF16),
        grid=(nq,),
        in_specs=[pl.BlockSpec(memory_space=pltpu.SMEM)] + [uspec(cb + c) for c in range(6)]
                 + [pl.BlockSpec((tq, tq), lambda i: (0, 0))],
        out_specs=pl.BlockSpec((tq, D_B), lambda i: (i, 0)),
        scratch_shapes=[pltpu.VMEM((nq, D_B, tq), BF16), pltpu.VMEM((nq, H_B // 2, tq, LANES), BF16),
                        pltpu.VMEM((H_B // 2, 2 * tq, LANES), BF16), pltpu.VMEM((H_B // 2, 2 * tq, LANES), F32),
                        pltpu.VMEM((H_B // 2, 2 * tq, LANES), F32), pltpu.VMEM((H_B // 2, 2 * tq, LANES), F32),
                        pltpu.VMEM((H_B // 2, 2 * tq, tq), F32), pltpu.VMEM((H_B // 2, 2 * tq, tq), BF16)],
        compiler_params=_params(("arbitrary",)),
        name="sb_prompt",
    )(sb_bias, u, u, u, u, u, u, upper)


Q_PAD = 8
SEQ_PER_STEP = 2


def _sb_sample_kernel(pt_ref, bias_ref, qkv_ref, u_ref, *refs, n_pages, steps):
    n_in = SEQ_PER_STEP * n_pages
    o_ref = refs[2 * n_in]
    for s in range(SEQ_PER_STEP):
        o_ref[s] = _sb_sample_one(qkv_ref[s], refs[s * n_pages:(s + 1) * n_pages],
                                  refs[n_in + s * n_pages:n_in + (s + 1) * n_pages], bias_ref, u_ref, n_pages, steps)


def _sb_sample_one(x, k_refs, v_refs, bias_ref, u_ref, n_pages, steps):
    scale = HD_B ** -0.5
    rows = H_B * Q_PAD
    q8 = x[:, 0:D_B] * scale
    zpad = jnp.zeros((PAGE_SIZE - Q_PAD, D_B), F32)
    k_new = jnp.concatenate([x[:, D_B:2 * D_B], zpad], axis=0).astype(BF16)
    v_new = jnp.concatenate([x[:, 2 * D_B:3 * D_B], zpad], axis=0).astype(BF16)
    lane_head = lax.broadcasted_iota(jnp.int32, (Q_PAD, D_B), 1) // HD_B
    qbd = jnp.concatenate([jnp.where(lane_head == h, q8, 0.0) for h in range(H_B)], axis=0).astype(BF16)
    bias = jnp.concatenate([jnp.full((Q_PAD, 1), bias_ref[h], F32) for h in range(H_B)], axis=0)

    page2d = lambda ref: ref[...].reshape(D_B, PAGE_SIZE).astype(BF16)
    z = jnp.concatenate([_dot(qbd, page2d(k_refs[p])) for p in range(n_pages)], axis=1) + bias
    zn = _dot_nt(qbd, k_new) + bias
    t_q = lax.broadcasted_iota(jnp.int32, (rows, PAGE_SIZE), 0) % Q_PAD
    c_k = lax.broadcasted_iota(jnp.int32, (rows, PAGE_SIZE), 1)
    new_ok = c_k < jnp.minimum(t_q, steps)

    lk = _neg_softplus(z)
    lkn_raw = _neg_softplus(zn)
    lkn = jnp.where(new_ok, lkn_raw, 0.0)
    upper = u_ref[...]
    later_n = _dot(lkn.astype(BF16), upper[:PAGE_SIZE, :PAGE_SIZE])
    att_n = jnp.where(new_ok, jnp.exp(zn + lkn_raw + later_n), 0.0)
    carry = jnp.sum(lkn, axis=-1, keepdims=True)
    blk = upper.shape[0]
    n_blk = n_pages * PAGE_SIZE // blk
    laters = [None] * n_blk
    for b in reversed(range(n_blk)):
        lkb = lk[:, b * blk:(b + 1) * blk]
        laters[b] = _dot(lkb.astype(BF16), upper) + carry
        carry = carry + jnp.sum(lkb, axis=-1, keepdims=True)
    att = jnp.exp(z + lk + jnp.concatenate(laters, axis=1)).astype(BF16)

    out = _dot(att_n.astype(BF16), v_new)
    for p in range(n_pages):
        out = out + _dot_nt(att[:, p * PAGE_SIZE:(p + 1) * PAGE_SIZE], page2d(v_refs[p]))
    y = jnp.zeros((Q_PAD, D_B), F32)
    for h in range(H_B):
        y = y + jnp.where(lane_head == h, out[h * Q_PAD:(h + 1) * Q_PAD], 0.0)
    return y


def _sb_sample(qkv_pad, cache_kt, cache_vt, page_table, sb_bias, layer, steps):
    batch, n_pages = page_table.shape
    blk = MXU_DIM
    upper = (lax.broadcasted_iota(jnp.int32, (blk, blk), 0) > lax.broadcasted_iota(jnp.int32, (blk, blk), 1)).astype(BF16)
    n_seq = SEQ_PER_STEP
    page = lambda s, p: pl.BlockSpec(
        (None, None, H_B, HD_B, PAGE_SIZE),
        functools.partial(lambda b, pt, s, p: (layer, pt[n_seq * b + s, p], 0, 0, 0), s=s, p=p))
    pages = [page(s, p) for s in range(n_seq) for p in range(n_pages)]
    grid_spec = pltpu.PrefetchScalarGridSpec(
        num_scalar_prefetch=1,
        grid=(batch // n_seq,),
        in_specs=[pl.BlockSpec(memory_space=pltpu.SMEM),
                  pl.BlockSpec((n_seq, Q_PAD, 3 * D_B), lambda b, pt: (b, 0, 0)),
                  pl.BlockSpec((blk, blk), lambda b, pt: (0, 0))] + pages * 2,
        out_specs=pl.BlockSpec((n_seq, Q_PAD, D_B), lambda b, pt: (b, 0, 0)),
    )
    return pl.pallas_call(
        functools.partial(_sb_sample_kernel, n_pages=n_pages, steps=steps),
        out_shape=jax.ShapeDtypeStruct((batch, Q_PAD, D_B), F32),
        grid_spec=grid_spec,
        compiler_params=_params(("parallel",)),
        name="sb_sample",
    )(page_table, sb_bias, qkv_pad, upper, *([cache_kt] * (n_seq * n_pages)), *([cache_vt] * (n_seq * n_pages)))


def _pool_mix(sums, x, cnts, wp_ref, scale_ref):
    outs = []
    for g in range(len(POOL_WINDOWS)):
        pooled = sums[g] / cnts[g] - x[:, g * C_POOL:(g + 1) * C_POOL]
        outs.append(_dot(pooled.astype(BF16), wp_ref[g].astype(BF16)))
    return jnp.concatenate(outs, axis=1) * scale_ref[...]


def _pool_prompt_kernel(u0_ref, u1_ref, wp_ref, scale_ref, o_ref, ext_ref):
    tp = u0_ref.shape[0]
    pad = 16
    i = pl.program_id(0)

    @pl.when(i == 0)
    def _():
        ext_ref[0:pad, :] = jnp.zeros((pad, D_C), F32)

    x = jnp.concatenate([u0_ref[...], u1_ref[...]], axis=1)
    ext_ref[pad:pad + tp, :] = x
    pos = i * tp + lax.broadcasted_iota(jnp.int32, (tp, 1), 0)
    sums, cnts = [], []
    for g, w in enumerate(POOL_WINDOWS):
        s = x[:, g * C_POOL:(g + 1) * C_POOL]
        for back in range(1, w):
            s = s + ext_ref[pad - back:pad - back + tp, g * C_POOL:(g + 1) * C_POOL]
        sums.append(s)
        cnts.append(jnp.minimum(w, pos + 1).astype(F32))
    o_ref[...] = _pool_mix(sums, x, cnts, wp_ref, scale_ref).astype(o_ref.dtype)
    ext_ref[0:pad, :] = x[tp - pad:, :]


def _pool_prompt(u, t, w_pool, pool_scale, tp):
    cb = (RWKV_COLS + 3 * D_B) // U_COL_BLOCK
    uspec = lambda c: pl.BlockSpec((tp, U_COL_BLOCK), functools.partial(lambda i, c: (i, c), c=c))
    return pl.pallas_call(
        _pool_prompt_kernel,
        out_shape=jax.ShapeDtypeStruct((t, D_C), BF16),
        grid=(t // tp,),
        in_specs=[uspec(cb), uspec(cb + 1),
                  pl.BlockSpec((len(POOL_WINDOWS), C_POOL, C_POOL), lambda i: (0, 0, 0)),
                  pl.BlockSpec((1, D_C), lambda i: (0, 0))],
        out_specs=pl.BlockSpec((tp, D_C), lambda i: (i, 0)),
        scratch_shapes=[pltpu.VMEM((16 + tp, D_C), F32)],
        compiler_params=_params(("arbitrary",)),
        name="pool_prompt",
    )(u, u, w_pool, pool_scale.reshape(1, D_C))


def _pool_sample_kernel(u0_ref, u1_ref, pre_ref, wp_ref, scale_ref, o_ref, *, steps, batch, start_pos):
    x_all = jnp.concatenate([u0_ref[...], u1_ref[...]], axis=1)
    ext = [pre_ref[r] for r in range(POOL_BUF)] + [x_all[t * batch:(t + 1) * batch] for t in range(steps)]
    for t in range(steps):
        sums, cnts = [], []
        for g, w in enumerate(POOL_WINDOWS):
            s = ext[POOL_BUF + t][:, g * C_POOL:(g + 1) * C_POOL]
            for back in range(1, w):
                s = s + ext[POOL_BUF + t - back][:, g * C_POOL:(g + 1) * C_POOL]
            sums.append(s)
            cnts.append(float(min(w, start_pos + t + 1)))
        o_ref[t * batch:(t + 1) * batch, :] = _pool_mix(sums, ext[POOL_BUF + t], cnts, wp_ref, scale_ref).astype(
            o_ref.dtype)


def _pool_sample(u, row_block0, prefix_tm, w_pool, pool_scale, steps, batch, start_pos):
    rows = steps * batch
    cb = (RWKV_COLS + 3 * D_B) // U_COL_BLOCK
    uspec = lambda c: pl.BlockSpec((rows, U_COL_BLOCK), functools.partial(lambda i, c: (row_block0, c), c=c))
    return pl.pallas_call(
        functools.partial(_pool_sample_kernel, steps=steps, batch=batch, start_pos=start_pos),
        out_shape=jax.ShapeDtypeStruct((rows, D_C), BF16),
        grid=(1,),
        in_specs=[uspec(cb), uspec(cb + 1),
                  pl.BlockSpec((POOL_BUF, batch, D_C), lambda i: (0, 0, 0)),
                  pl.BlockSpec((len(POOL_WINDOWS), C_POOL, C_POOL), lambda i: (0, 0, 0)),
                  pl.BlockSpec((1, D_C), lambda i: (0, 0))],
        out_specs=pl.BlockSpec((rows, D_C), lambda i: (0, 0)),
        compiler_params=_params(("arbitrary",)),
        name="pool_sample",
    )(u, u, prefix_tm, w_pool, pool_scale.reshape(1, D_C))


DENSE_SUB_ROWS = 272
TM_DENSE = 1088
TM_WIDE = 2176
TM_DOWN = 544


def kernel(x_prompt, x_sample, p_prompt, p_sample, cache_k, cache_v, page_table, state_wkv, state_shift, state_pool, g_mix, w_in, mu_shift, w0, w_lora_up, a0, a_lora_up, g_lora_up, k_k, k_a, r_k, ln_w, ln_b, sb_bias, w_pool, pool_scale, w_o, g_ffn, w_gate, w_up, w_down, g_ple, w_pg, w_pp, g_final):
    depth = w_in.shape[0]
    tp = x_prompt.shape[1]
    bs, ts = x_sample.shape[0], x_sample.shape[1]
    rows_s = bs * ts
    m = tp + rows_s
    n_pages = page_table.shape[1]
    past_len = n_pages * PAGE_SIZE

    to_tm = lambda x: jnp.swapaxes(x, 0, 1).reshape(rows_s, x.shape[-1])
    from_tm = lambda x: jnp.swapaxes(x.reshape(ts, bs, x.shape[-1]), 0, 1)

    h = jnp.concatenate([x_prompt[0], to_tm(x_sample)], axis=0)
    cache_kt = jnp.transpose(cache_k, (0, 1, 3, 4, 2))
    cache_vt = jnp.transpose(cache_v, (0, 1, 3, 4, 2))
    state_t = jnp.transpose(state_wkv, (0, 2, 3, 4, 1))

    outs = {n: [] for n in ("kp", "vp", "ks", "vs", "wp", "ws", "sp", "ss", "pp", "ps")}
    for i in range(depth):
        lw = dict(mu_shift=mu_shift[i], w0=w0[i], w_lora_up=w_lora_up[i], a0=a0[i], a_lora_up=a_lora_up[i],
                  g_lora_up=g_lora_up[i], k_k=k_k[i], k_a=k_a[i], r_k=r_k[i].reshape(-1), ln_w=ln_w[i], ln_b=ln_b[i])
        xn = _rmsnorm(h, g_mix[i], BF16, TM_DENSE)
        u = _dense([(xn, w_in, D_MODEL, 0, 0)], i, 1, _epi_plain, None, TM_WIDE, 768, F32, "dense_in")

        prep_p = _rwkv_prep(u, 0, tp, jnp.zeros((1, RWKV_COLS), F32), 1, 256, lw)
        ya_p, wkv_p = _rwkv_scan_prompt(*prep_p[:6], prep_p[7], prep_p[6], lw["ln_w"], lw["ln_b"])
        prep_s = _rwkv_prep(u, tp // rows_s, rows_s, state_shift[i], bs, rows_s, lw)
        y_s, wkv_s = _rwkv_scan_sample(*prep_s[:6], state_t, i, ts, bs)
        ya_s = _rwkv_post(y_s, prep_s[7], prep_s[6], lw, rows_s)

        yb_p = _sb_prompt(u, tp, sb_bias[i])
        qkv_s = from_tm(u[tp:, RWKV_COLS:RWKV_COLS + 3 * D_B])
        qkv_pad = jnp.pad(qkv_s, ((0, 0), (0, Q_PAD - ts), (0, 0)))
        yb_s = _sb_sample(qkv_pad, cache_kt, cache_vt, page_table, sb_bias[i], i, ts)
        yb_s = to_tm(yb_s[:, :ts]).astype(BF16)

        yc_p = _pool_prompt(u, tp, w_pool[i], pool_scale[i], 512)
        yc_s = _pool_sample(u, tp // rows_s, jnp.swapaxes(state_pool[i], 0, 1), w_pool[i], pool_scale[i],
                            ts, bs, past_len)

        ya = jnp.concatenate([ya_p, ya_s], axis=0)
        yb = jnp.concatenate([yb_p, yb_s], axis=0)
        yc = jnp.concatenate([yc_p, yc_s], axis=0)
        h = _dense([(ya, w_o, D_A, 0, 0), (yb, w_o, D_B, D_A // D_B, 0),
                    (yc, w_o, D_C, (D_A + D_B) // D_C, 0)], i, 1, _epi_residual, h, TM_WIDE, 512, F32, "dense_o")

        xn = _rmsnorm(h, g_ffn[i], BF16, TM_DENSE)
        act = _dense([(xn, w_gate, D_MODEL, 0, 0), (xn, w_up, D_MODEL, 0, 1)], i, 2, _epi_swiglu, None,
                     TM_WIDE, 512, BF16, "dense_ffn_in")
        h = _dense([(act, w_down, D_FF, 0, 0)], i, 1, _epi_residual, h, TM_DOWN, 512, F32, "dense_ffn_out")

        xn = _rmsnorm(h, g_ple[i], BF16, TM_DENSE)
        p_i = jnp.concatenate([p_prompt[i, 0], to_tm(p_sample[i])], axis=0).astype(BF16)
        h = _dense([(xn, w_pg, D_MODEL, 0, 0), (p_i, w_pp, D_PLE, 0, 1)], i, 2, _epi_gated_residual, h,
                   TM_WIDE, 512, F32, "dense_ple")

        kv_p = u[:tp, RWKV_COLS + D_B:RWKV_COLS + 3 * D_B]
        outs["kp"].append(kv_p[:, :D_B].reshape(1, tp, H_B, HD_B))
        outs["vp"].append(kv_p[:, D_B:].reshape(1, tp, H_B, HD_B))
        outs["ks"].append(qkv_s[..., D_B:2 * D_B].reshape(bs, ts, H_B, HD_B))
        outs["vs"].append(qkv_s[..., 2 * D_B:].reshape(bs, ts, H_B, HD_B))
        outs["wp"].append(wkv_p.reshape(1, H_A, HD_A, HD_A))
        outs["ws"].append(jnp.transpose(wkv_s, (3, 0, 1, 2)))
        outs["sp"].append(u[tp - 1:tp, :RWKV_COLS])
        outs["ss"].append(u[m - bs:, :RWKV_COLS])
        uc_s = from_tm(u[tp:, RWKV_COLS + 3 * D_B:])
        outs["pp"].append(u[tp - POOL_BUF:tp, RWKV_COLS + 3 * D_B:][None])
        outs["ps"].append(jnp.concatenate([state_pool[i][:, ts:], uc_s], axis=1))

    y_prompt = _rmsnorm(h, g_final, F32, rows_s, 0, tp)[None]
    y_sample = from_tm(_rmsnorm(h, g_final, F32, rows_s, tp // rows_s, rows_s))
    stack = lambda n: jnp.stack(outs[n])
    return (y_prompt, y_sample, stack("kp"), stack("vp"), stack("ks"), stack("vs"), stack("wp"), stack("ws"),
            stack("sp"), stack("ss"), stack("pp"), stack("ps"))
```
